```python
import math
import functools
import jax
import jax.numpy as jnp
from jax import lax
import numpy as np

D_MODEL = 1024
BATCH = 4
SEQ = 4096
DEPTH = 4
DEC_BATCH = 128
DEC_SEQ = 8
PAST_LEN = 8192
PAGE_SIZE = 128

HEAD_DIM = 64
MLA_HEADS = 8
MLA_NOPE = 64
MLA_ROPE = 32
MLA_V = 64
Q_LORA = 256
KV_LORA = 256
ROPE_THETA = 10000.0
MLA_SCALE = (MLA_NOPE + MLA_ROPE) ** -0.5
MOBA_HEADS = 8
MOBA_KV_HEADS = 2
MOBA_GROUP = MOBA_HEADS // MOBA_KV_HEADS
MOBA_BLOCK = 256
MOBA_TOPK = 3
FOX_HEADS = 16
FOX_KV_HEADS = 4
FOX_GROUP = FOX_HEADS // FOX_KV_HEADS
REL_BUCKETS = 32
REL_MAX_DIST = 128
D_FF = 4 * D_MODEL
Q_BLOCK = 128
EPS = 1e-6
NEG = -1e30
N_EVEN = (DEPTH + 1) // 2
N_ODD = DEPTH // 2
EVEN_IN = Q_LORA + KV_LORA + MLA_ROPE + MOBA_HEADS * HEAD_DIM + 2 * MOBA_KV_HEADS * HEAD_DIM
EVEN_SPLITS = (Q_LORA, Q_LORA + KV_LORA, Q_LORA + KV_LORA + MLA_ROPE,
               Q_LORA + KV_LORA + MLA_ROPE + MOBA_HEADS * HEAD_DIM,
               Q_LORA + KV_LORA + MLA_ROPE + MOBA_HEADS * HEAD_DIM + MOBA_KV_HEADS * HEAD_DIM)
EVEN_OUT = MLA_HEADS * MLA_V + MOBA_HEADS * HEAD_DIM
ODD_IN = FOX_HEADS * HEAD_DIM + 2 * FOX_KV_HEADS * HEAD_DIM + FOX_HEADS
ODD_SPLITS = (FOX_HEADS * HEAD_DIM, FOX_HEADS * HEAD_DIM + FOX_KV_HEADS * HEAD_DIM,
              FOX_HEADS * HEAD_DIM + 2 * FOX_KV_HEADS * HEAD_DIM)
ODD_OUT = FOX_HEADS * HEAD_DIM

kernel_name = "hybrid_mla_moba_fox_decode_step"


def rms_norm(x, g):
    xf = x.astype(jnp.float32)
    y = xf * lax.rsqrt(jnp.mean(xf * xf, axis=-1, keepdims=True) + EPS)
    return (y * g.astype(jnp.float32)).astype(x.dtype)


def rope(x, pos):
    half = x.shape[-1] // 2
    inv = ROPE_THETA ** (-jnp.arange(half, dtype=jnp.float32) / half)
    ang = pos.astype(jnp.float32)[:, None] * inv[None, :]
    shape = (pos.shape[0],) + (1,) * (x.ndim - 3) + (half,)
    cos = jnp.cos(ang).reshape(shape)
    sin = jnp.sin(ang).reshape(shape)
    xf = x.astype(jnp.float32)
    x1, x2 = xf[..., :half], xf[..., half:]
    return jnp.concatenate([x1 * cos - x2 * sin, x1 * sin + x2 * cos], axis=-1).astype(x.dtype)


def t5_bucket(rel):
    n = jnp.maximum(rel, 0)
    exact = REL_BUCKETS // 2
    scaled = jnp.log(jnp.maximum(n, 1).astype(jnp.float32) / exact) / math.log(REL_MAX_DIST / exact)
    large = exact + (scaled * (REL_BUCKETS - exact)).astype(jnp.int32)
    return jnp.where(n < exact, n, jnp.minimum(large, REL_BUCKETS - 1))


def gather_pages(pool, page_table):
    g = pool[page_table]
    return g.reshape((g.shape[0], g.shape[1] * g.shape[2]) + g.shape[3:])


def sweep_queries(fn, qs, pos):
    T = pos.shape[0]
    nq = T // Q_BLOCK

    def split(a):
        return jnp.moveaxis(a.reshape((a.shape[0], nq, Q_BLOCK) + a.shape[2:]), 1, 0)

    out = lax.map(lambda args: fn(*args), tuple(split(a) for a in qs) + (pos.reshape(nq, Q_BLOCK),))
    out = jnp.moveaxis(out, 0, 1)
    return out.reshape((out.shape[0], T) + out.shape[3:])


def even_project(h, pos, w_in, g_q_lat, g_kv_lat, w_uq, g_mla_q, g_mla_k, g_moba_q, g_moba_k):
    B, T = h.shape[0], h.shape[1]
    q_lat, c_kv, k_rope, mq, mk, mv = jnp.split(h @ w_in, EVEN_SPLITS, axis=-1)
    c_kv = rms_norm(c_kv, g_kv_lat)
    q = jnp.einsum('btc,chd->bthd', rms_norm(q_lat, g_q_lat), w_uq)
    q = rms_norm(q, g_mla_q)
    qn = q[..., :MLA_NOPE] * g_mla_k[:MLA_NOPE]
    qr = rope(q[..., MLA_NOPE:], pos)
    mq = rms_norm(mq.reshape(B, T, MOBA_HEADS, HEAD_DIM), g_moba_q)
    mk = rms_norm(mk.reshape(B, T, MOBA_KV_HEADS, HEAD_DIM), g_moba_k)
    mv = mv.reshape(B, T, MOBA_KV_HEADS, HEAD_DIM)
    return qn, qr, c_kv, k_rope, mq, mk, mv


def mla_keys(c, k_rope, w_uk, g_k):
    L = c.shape[1]
    kn = jnp.einsum('blc,chd->blhd', c, w_uk)
    ss = jnp.sum(jnp.square(kn.astype(jnp.float32)), -1) + jnp.sum(jnp.square(k_rope.astype(jnp.float32)), -1)[..., None]
    inv = lax.rsqrt(ss / (MLA_NOPE + MLA_ROPE) + EPS)
    krr = rope(k_rope * g_k[MLA_NOPE:], jnp.arange(L, dtype=jnp.int32))
    return kn, krr, inv


def mla_attend(qn, qr, q_pos, kn, krr, inv, c, w_uv):
    L = c.shape[1]
    s = jnp.einsum('bthd,blhd->bhtl', qn, kn) + jnp.einsum('bthr,blr->bhtl', qr, krr)
    s = s.astype(jnp.float32) * (jnp.moveaxis(inv, 1, 2)[:, :, None, :] * MLA_SCALE)
    causal = jnp.arange(L)[None, :] <= q_pos[:, None]
    s = jnp.where(causal[None, None], s, NEG)
    p = jax.nn.softmax(s, axis=-1).astype(c.dtype)
    pc = jnp.einsum('bhtl,blc->bthc', p, c)
    return jnp.einsum('bthc,chd->bthd', pc, w_uv)


def moba_prepare(k, v):
    B, L = k.shape[0], k.shape[1]
    nblk = -(-L // MOBA_BLOCK)
    pad = ((0, 0), (0, nblk * MOBA_BLOCK - L), (0, 0), (0, 0))

    def blocks(a):
        a = jnp.pad(a, pad).reshape(B, nblk, MOBA_BLOCK, MOBA_KV_HEADS, HEAD_DIM)
        return a.transpose(0, 3, 1, 2, 4)

    kbt, vbt = blocks(k), blocks(v)
    kmean = jnp.mean(kbt.astype(jnp.float32), axis=3).astype(k.dtype)
    return kbt, vbt, kmean


def moba_attend(q, q_pos, kbt, vbt, kmean, rel_bias):
    B, T = q.shape[0], q.shape[1]
    nblk = kbt.shape[2]
    n_sel = min(MOBA_TOPK, nblk)
    qg = q.reshape(B, T, MOBA_KV_HEADS, MOBA_GROUP, HEAD_DIM)
    own = q_pos // MOBA_BLOCK
    gate = jnp.einsum('btkgd,bknd->btkgn', qg, kmean).astype(jnp.float32)
    past_blk = jnp.arange(nblk)[None, :] < own[:, None]
    gate = jnp.where(past_blk[None, :, None, None, :], gate, NEG)
    _, sel = lax.top_k(gate, n_sel)
    sel_ok = jnp.arange(n_sel)[None, :] < own[:, None]
    bi = jnp.arange(B)[:, None, None, None, None]
    ki = jnp.arange(MOBA_KV_HEADS)[None, None, :, None, None]
    k_sel = kbt[bi, ki, sel]
    v_sel = vbt[bi, ki, sel]
    k_own = kbt[:, :, own]
    v_own = vbt[:, :, own]
    scale = HEAD_DIM ** -0.5
    l_sel = jnp.einsum('btkgd,btkgrid->btkgri', qg, k_sel).astype(jnp.float32) * scale
    l_own = jnp.einsum('btkgd,bktid->btkgi', qg, k_own).astype(jnp.float32) * scale
    rb = rel_bias.astype(jnp.float32).reshape(REL_BUCKETS, MOBA_KV_HEADS, MOBA_GROUP)
    off = jnp.arange(MOBA_BLOCK)
    rel_sel = q_pos[None, :, None, None, None, None] - (sel[..., None] * MOBA_BLOCK + off)
    hk = jnp.arange(MOBA_KV_HEADS)[None, None, :, None, None, None]
    hg = jnp.arange(MOBA_GROUP)[None, None, None, :, None, None]
    l_sel = l_sel + rb[t5_bucket(rel_sel), hk, hg]
    l_sel = jnp.where(sel_ok[None, :, None, None, :, None], l_sel, NEG)
    rel_own = q_pos[:, None] - (own[:, None] * MOBA_BLOCK + off)
    l_own = l_own + rb[t5_bucket(rel_own)].transpose(0, 2, 3, 1)[None]
    l_own = jnp.where((rel_own >= 0)[None, :, None, None, :], l_own, NEG)
    logits = jnp.concatenate([l_sel.reshape(B, T, MOBA_KV_HEADS, MOBA_GROUP, n_sel * MOBA_BLOCK), l_own], axis=-1)
    p = jax.nn.softmax(logits, axis=-1).astype(vbt.dtype)
    p_sel = p[..., :n_sel * MOBA_BLOCK].reshape(B, T, MOBA_KV_HEADS, MOBA_GROUP, n_sel, MOBA_BLOCK)
    p_own = p[..., n_sel * MOBA_BLOCK:]
    o = jnp.einsum('btkgri,btkgrid->btkgd', p_sel, v_sel) + jnp.einsum('btkgi,bktid->btkgd', p_own, v_own)
    return o.reshape(B, T, MOBA_HEADS * HEAD_DIM)


def odd_project(h, w_in, b_f, g_q, g_k):
    B, T = h.shape[0], h.shape[1]
    q, k, v, f = jnp.split(h @ w_in, ODD_SPLITS, axis=-1)
    q = rms_norm(q.reshape(B, T, FOX_KV_HEADS, FOX_GROUP, HEAD_DIM), g_q)
    k = rms_norm(k.reshape(B, T, FOX_KV_HEADS, HEAD_DIM), g_k)
    v = v.reshape(B, T, FOX_KV_HEADS, HEAD_DIM)
    logf = jax.nn.log_sigmoid((f + b_f).astype(jnp.float32))
    return q, k, v, logf


def fox_attend(q, qc, q_pos, k, v, kc):
    B, T, L = q.shape[0], q.shape[1], k.shape[1]
    s = jnp.einsum('btkgd,blkd->bkgtl', q, k).astype(jnp.float32) * (HEAD_DIM ** -0.5)
    qc_t = qc.reshape(B, T, FOX_KV_HEADS, FOX_GROUP).transpose(0, 2, 3, 1)[..., None]
    kc_t = kc.reshape(B, L, FOX_KV_HEADS, FOX_GROUP).transpose(0, 2, 3, 1)[:, :, :, None, :]
    s = s + (qc_t - kc_t)
    causal = jnp.arange(L)[None, :] <= q_pos[:, None]
    s = jnp.where(causal, s, NEG)
    p = jax.nn.softmax(s, axis=-1).astype(v.dtype)
    o = jnp.einsum('bkgtl,blkd->btkgd', p, v)
    return o.reshape(B, T, FOX_HEADS * HEAD_DIM)


def sq_relu_mlp(h, w_up, w_down):
    u = jax.nn.relu(h @ w_up)
    return (u * u) @ w_down


def setup_inputs(seed: int = 0) -> dict:
    key = jax.random.key(seed)
    ks = iter(jax.random.split(key, 48))
    f32 = jnp.float32

    def nrm(shape, scale):
        return jax.random.normal(next(ks), shape, f32) * scale

    def gain(shape):
        return 1.0 + 0.05 * jax.random.normal(next(ks), shape, f32)

    n_pages = PAST_LEN // PAGE_SIZE
    n_used = DEC_BATCH * n_pages
    n_pool = n_used + (n_used + 3) // 4
    perm = jax.random.permutation(next(ks), n_pool)
    page_table = perm[:n_used].reshape(DEC_BATCH, n_pages).astype(jnp.int32)
    d = D_MODEL
    return {
        'x_prompt': nrm((BATCH, SEQ, d), 1.0),
        'x_sample': nrm((DEC_BATCH, DEC_SEQ, d), 1.0),
        'cache_mla_latent': nrm((N_EVEN, n_pool, PAGE_SIZE, KV_LORA), 1.0),
        'cache_mla_krope': nrm((N_EVEN, n_pool, PAGE_SIZE, MLA_ROPE), 1.0),
        'cache_moba_k': nrm((N_EVEN, n_pool, PAGE_SIZE, MOBA_KV_HEADS, HEAD_DIM), 1.0),
        'cache_moba_v': nrm((N_EVEN, n_pool, PAGE_SIZE, MOBA_KV_HEADS, HEAD_DIM), 1.0),
        'cache_fox_k': nrm((N_ODD, n_pool, PAGE_SIZE, FOX_KV_HEADS, HEAD_DIM), 1.0),
        'cache_fox_v': nrm((N_ODD, n_pool, PAGE_SIZE, FOX_KV_HEADS, HEAD_DIM), 1.0),
        'cache_fox_logf': jax.nn.log_sigmoid(nrm((N_ODD, n_pool, PAGE_SIZE, FOX_HEADS), 1.0)),
        'page_table': page_table,
        'rel_bias': nrm((REL_BUCKETS, MOBA_HEADS), 0.5),
        'g_attn': gain((DEPTH, d)),
        'g_mlp': gain((DEPTH, d)),
        'w_mlp_up': nrm((DEPTH, d, D_FF), d ** -0.5),
        'w_mlp_down': nrm((DEPTH, D_FF, d), D_FF ** -0.5),
        'w_in_even': nrm((N_EVEN, d, EVEN_IN), d ** -0.5),
        'g_q_lat': gain((N_EVEN, Q_LORA)),
        'g_kv_lat': gain((N_EVEN, KV_LORA)),
        'w_uq': nrm((N_EVEN, Q_LORA, MLA_HEADS, MLA_NOPE + MLA_ROPE), Q_LORA ** -0.5),
        'w_uk': nrm((N_EVEN, KV_LORA, MLA_HEADS, MLA_NOPE), KV_LORA ** -0.5),
        'w_uv': nrm((N_EVEN, KV_LORA, MLA_HEADS, MLA_V), KV_LORA ** -0.5),
        'g_mla_q': gain((N_EVEN, MLA_NOPE + MLA_ROPE)),
        'g_mla_k': gain((N_EVEN, MLA_NOPE + MLA_ROPE)),
        'g_moba_q': gain((N_EVEN, HEAD_DIM)),
        'g_moba_k': gain((N_EVEN, HEAD_DIM)),
        'w_o_even': nrm((N_EVEN, EVEN_OUT, d), EVEN_OUT ** -0.5),
        'w_in_odd': nrm((N_ODD, d, ODD_IN), d ** -0.5),
        'b_forget': nrm((N_ODD, FOX_HEADS), 0.1),
        'g_fox_q': gain((N_ODD, HEAD_DIM)),
        'g_fox_k': gain((N_ODD, HEAD_DIM)),
        'w_o_odd': nrm((N_ODD, ODD_OUT, d), ODD_OUT ** -0.5),
    }


def reference(x_prompt, x_sample, cache_mla_latent, cache_mla_krope, cache_moba_k, cache_moba_v,
              cache_fox_k, cache_fox_v, cache_fox_logf, page_table, rel_bias, g_attn, g_mlp,
              w_mlp_up, w_mlp_down, w_in_even, g_q_lat, g_kv_lat, w_uq, w_uk, w_uv, g_mla_q, g_mla_k,
              g_moba_q, g_moba_k, w_o_even, w_in_odd, b_forget, g_fox_q, g_fox_k, w_o_odd):
    B, S = x_prompt.shape[0], x_prompt.shape[1]
    DB, Tn = x_sample.shape[0], x_sample.shape[1]
    past = page_table.shape[1] * cache_mla_latent.shape[2]
    pos_p = jnp.arange(S, dtype=jnp.int32)
    pos_s = past + jnp.arange(Tn, dtype=jnp.int32)
    xp, xs = x_prompt, x_sample
    lat_p, lat_s, kr_p, kr_s = [], [], [], []
    mk_p, mk_s, mv_p, mv_s = [], [], [], []
    fk_p, fk_s, fv_p, fv_s, lf_p, lf_s = [], [], [], [], [], []
    for layer in range(DEPTH):
        hp = rms_norm(xp, g_attn[layer])
        hs = rms_norm(xs, g_attn[layer])
        if layer % 2 == 0:
            e = layer // 2
            proj = functools.partial(even_project, w_in=w_in_even[e], g_q_lat=g_q_lat[e], g_kv_lat=g_kv_lat[e],
                                     w_uq=w_uq[e], g_mla_q=g_mla_q[e], g_mla_k=g_mla_k[e],
                                     g_moba_q=g_moba_q[e], g_moba_k=g_moba_k[e])
            qn_p, qr_p, c_p, r_p, q2_p, k2_p, v2_p = proj(hp, pos_p)
            qn_s, qr_s, c_s, r_s, q2_s, k2_s, v2_s = proj(hs, pos_s)
            kn, krr, inv = mla_keys(c_p, r_p, w_uk[e], g_mla_k[e])
            o1_p = sweep_queries(functools.partial(mla_attend, kn=kn, krr=krr, inv=inv, c=c_p, w_uv=w_uv[e]),
                                 (qn_p, qr_p), pos_p)
            c_all = jnp.concatenate([gather_pages(cache_mla_latent[e], page_table), c_s], axis=1)
            r_all = jnp.concatenate([gather_pages(cache_mla_krope[e], page_table), r_s], axis=1)
            kn, krr, inv = mla_keys(c_all, r_all, w_uk[e], g_mla_k[e])
            o1_s = mla_attend(qn_s, qr_s, pos_s, kn, krr, inv, c_all, w_uv[e])
            kbt, vbt, km = moba_prepare(k2_p, v2_p)
            o2_p = sweep_queries(functools.partial(moba_attend, kbt=kbt, vbt=vbt, kmean=km, rel_bias=rel_bias),
                                 (q2_p,), pos_p)
            k_all = jnp.concatenate([gather_pages(cache_moba_k[e], page_table), k2_s], axis=1)
            v_all = jnp.concatenate([gather_pages(cache_moba_v[e], page_table), v2_s], axis=1)
            kbt, vbt, km = moba_prepare(k_all, v_all)
            o2_s = moba_attend(q2_s, pos_s, kbt, vbt, km, rel_bias)
            mix_p = jnp.concatenate([o1_p.reshape(B, S, MLA_HEADS * MLA_V), o2_p], axis=-1) @ w_o_even[e]
            mix_s = jnp.concatenate([o1_s.reshape(DB, Tn, MLA_HEADS * MLA_V), o2_s], axis=-1) @ w_o_even[e]
            lat_p.append(c_p); lat_s.append(c_s); kr_p.append(r_p); kr_s.append(r_s)
            mk_p.append(k2_p); mk_s.append(k2_s); mv_p.append(v2_p); mv_s.append(v2_s)
        else:
            j = layer // 2
            q_p, k_p, v_p, lfp = odd_project(hp, w_in_odd[j], b_forget[j], g_fox_q[j], g_fox_k[j])
            q_s, k_s, v_s, lfs = odd_project(hs, w_in_odd[j], b_forget[j], g_fox_q[j], g_fox_k[j])
            cum_p = jnp.cumsum(lfp, axis=1)
            o_p = sweep_queries(functools.partial(fox_attend, k=k_p, v=v_p, kc=cum_p), (q_p, cum_p), pos_p)
            k_all = jnp.concatenate([gather_pages(cache_fox_k[j], page_table), k_s], axis=1)
            v_all = jnp.concatenate([gather_pages(cache_fox_v[j], page_table), v_s], axis=1)
            lf_all = jnp.concatenate([gather_pages(cache_fox_logf[j], page_table).astype(jnp.float32), lfs], axis=1)
            cum_all = jnp.cumsum(lf_all, axis=1)
            o_s = fox_attend(q_s, cum_all[:, past:], pos_s, k_all, v_all, cum_all)
            mix_p = o_p @ w_o_odd[j]
            mix_s = o_s @ w_o_odd[j]
            fk_p.append(k_p); fk_s.append(k_s); fv_p.append(v_p); fv_s.append(v_s)
            lf_p.append(lfp); lf_s.append(lfs)
        xp = xp + mix_p
        xs = xs + mix_s
        xp = xp + sq_relu_mlp(rms_norm(xp, g_mlp[layer]), w_mlp_up[layer], w_mlp_down[layer])
        xs = xs + sq_relu_mlp(rms_norm(xs, g_mlp[layer]), w_mlp_up[layer], w_mlp_down[layer])
    new_lat_p = jnp.stack(lat_p)
    new_lat_s = jnp.stack(lat_s)
    new_kr_p = jnp.stack(kr_p)
    new_kr_s = jnp.stack(kr_s)
    new_mk_p = jnp.stack(mk_p)
    new_mk_s = jnp.stack(mk_s)
    new_mv_p = jnp.stack(mv_p)
    new_mv_s = jnp.stack(mv_s)
    new_fk_p = jnp.stack(fk_p)
    new_fk_s = jnp.stack(fk_s)
    new_fv_p = jnp.stack(fv_p)
    new_fv_s = jnp.stack(fv_s)
    new_lf_p = jnp.stack(lf_p)
    new_lf_s = jnp.stack(lf_s)
    return (xp, xs, new_lat_p, new_lat_s, new_kr_p, new_kr_s, new_mk_p, new_mk_s, new_mv_p, new_mv_s,
            new_fk_p, new_fk_s, new_fv_p, new_fv_s, new_lf_p, new_lf_s)
```

```python
import functools
import math

import jax
import jax.numpy as jnp
import numpy as np
from jax import lax
from jax.experimental import pallas as pl
from jax.experimental.pallas import tpu as pltpu

F32 = jnp.float32
BF16 = jnp.bfloat16
NEG = -1e30
EPS = 1e-6
ROPE_THETA = 10000.0
HEAD_DIM = 64
MOBA_BLOCK = 256
MOBA_TOPK = 3
REL_BUCKETS = 32
REL_MAX_DIST = 128
LANES = 128
TOKEN_TILE = 512
FF_TILE = 1024
CHUNK_PAGES = 16
VMEM_LIMIT = 56 * 1024 * 1024


def _cp(sem, vmem=VMEM_LIMIT):
    return pltpu.CompilerParams(dimension_semantics=sem, vmem_limit_bytes=vmem)


def _dot(a, b):
    return jnp.dot(a.astype(BF16), b.astype(BF16), preferred_element_type=F32)


def _dot_nt(a, b):
    return lax.dot_general(a.astype(BF16), b.astype(BF16), (((1,), (1,)), ((), ())),
                           preferred_element_type=F32)


def _dot_tn(a, b):
    return lax.dot_general(a.astype(BF16), b.astype(BF16), (((0,), (0,)), ((), ())),
                           preferred_element_type=F32)


def _split(a, n):
    parts = []
    r = a.astype(F32)
    for _ in range(n):
        p = r.astype(BF16)
        parts.append(p)
        r = r - p.astype(F32)
    return parts


def _dot_exact_rhs(a, b, n=3):
    return sum(jnp.dot(p, b, preferred_element_type=F32) for p in _split(a, n))


def _dot_nt_exact_lhs(a, b, n=3):
    return sum(lax.dot_general(a, p, (((1,), (1,)), ((), ())), preferred_element_type=F32)
               for p in _split(b, n))


def _rms(v, g):
    return v * lax.rsqrt(jnp.mean(v * v, axis=-1, keepdims=True) + EPS) * g


def _head_rms(v, g, bd, width):
    step = min(width, bd.shape[0])
    outs = []
    for s in range(0, width, step):
        blk = v[:, s:s + step]
        ss = _dot_exact_rhs(blk * blk, bd[:step, :step], 2)
        outs.append(blk * lax.rsqrt(ss * (1.0 / HEAD_DIM) + EPS))
    out = outs[0] if len(outs) == 1 else jnp.concatenate(outs, axis=-1)
    return out * g


def _rot_block(v, cos_t, sin_t):
    lane = lax.broadcasted_iota(jnp.int32, v.shape, 1)
    partner = jnp.where(lane < 80, pltpu.roll(v, LANES - 16, 1), pltpu.roll(v, 16, 1))
    return v * cos_t + partner * sin_t


def _col_from_row(row, eye):
    return jnp.sum(eye * row, axis=-1, keepdims=True)


def _even_proj_kernel(x_ref, ga_ref, wp_ref, gql_ref, gkv_ref, wuq_ref, gq_ref, gkn_ref, gkr_ref,
                      wuk_ref, wuv_ref, gmq_ref, gmk_ref, bd_ref, cos_ref, sin_ref,
                      c_ref, krb_ref, mk_ref, mv_ref, qmla_ref, kmla_ref, vmla_ref,
                      qmo_ref, kmo_ref, vmo_ref, kmean_ref, *, n_heads, qk_dim, scale):
    x = x_ref[...]
    h = _rms(x, ga_ref[...]).astype(BF16)
    y = jnp.dot(h, wp_ref[...], preferred_element_type=F32)
    cos_t = cos_ref[...]
    sin_t = sin_ref[...]
    bd = bd_ref[...]

    c = _rms(y[:, 256:512], gkv_ref[...])
    c_ref[...] = c
    cb = c.astype(BF16)

    ql = _rms(y[:, 0:256], gql_ref[...]).astype(BF16)
    q = jnp.dot(ql, wuq_ref[...], preferred_element_type=F32)
    gq = gq_ref[...]
    gkn = gkn_ref[...]
    inv_d = 1.0 / qk_dim
    for hh in range(n_heads):
        sl = slice(LANES * hh, LANES * (hh + 1))
        qh = q[:, sl]
        qh = qh * lax.rsqrt(jnp.sum(qh * qh, axis=-1, keepdims=True) * inv_d + EPS) * gq
        qmla_ref[:, sl] = _rot_block(qh * gkn, cos_t, sin_t).astype(BF16)

    krb = y[:, 1280:1408]
    krb_ref[...] = krb
    ssr = jnp.sum(krb * krb, axis=-1, keepdims=True)
    krr = _rot_block(krb * gkr_ref[...], cos_t, sin_t)
    kn = jnp.dot(cb, wuk_ref[...], preferred_element_type=F32)
    for hh in range(n_heads):
        sl = slice(LANES * hh, LANES * (hh + 1))
        kh = kn[:, sl]
        inv = lax.rsqrt((jnp.sum(kh * kh, axis=-1, keepdims=True) + ssr) * inv_d + EPS) * scale
        kmla_ref[:, sl] = ((kh + krr) * inv).astype(BF16)
    vmla_ref[...] = jnp.dot(cb, wuv_ref[...], preferred_element_type=F32).astype(BF16)

    mq = _head_rms(y[:, 512:1024], gmq_ref[...], bd, 512)
    qmo_ref[...] = mq.astype(BF16)
    mk = _head_rms(y[:, 1024:1152], gmk_ref[...], bd, 128)
    mk_ref[...] = mk
    kmo_ref[...] = mk.astype(BF16)
    mv = y[:, 1152:1280]
    mv_ref[...] = mv
    vmo_ref[...] = mv.astype(BF16)
    tm = x.shape[0]
    for i in range(tm // MOBA_BLOCK):
        kmean_ref[0, i:i + 1, :] = jnp.mean(mk[MOBA_BLOCK * i:MOBA_BLOCK * (i + 1)], axis=0, keepdims=True)


def _even_proj(x, ga, wp, gql, gkv, wuq, gq, gkn, gkr, wuk, wuv, gmq, gmk, bd, cos_t, sin_t, *, tm, qk_dim, scale):
    n, d = x.shape
    nt = n // tm
    row = lambda c: pl.BlockSpec((tm, c), lambda i: (i, 0))
    full = lambda a: pl.BlockSpec(a.shape, lambda i: (0,) * a.ndim)
    consts = (ga, wp, gql, gkv, wuq, gq, gkn, gkr, wuk, wuv, gmq, gmk, bd)
    out_shape = (
        jax.ShapeDtypeStruct((n, 256), F32), jax.ShapeDtypeStruct((n, 128), F32),
        jax.ShapeDtypeStruct((n, 128), F32), jax.ShapeDtypeStruct((n, 128), F32),
        jax.ShapeDtypeStruct((n, 1024), BF16), jax.ShapeDtypeStruct((n, 1024), BF16),
        jax.ShapeDtypeStruct((n, 512), BF16), jax.ShapeDtypeStruct((n, 512), BF16),
        jax.ShapeDtypeStruct((n, 128), BF16), jax.ShapeDtypeStruct((n, 128), BF16),
        jax.ShapeDtypeStruct((nt, tm // MOBA_BLOCK, 128), F32),
    )
    out_specs = (row(256), row(128), row(128), row(128), row(1024), row(1024), row(512), row(512),
                 row(128), row(128), pl.BlockSpec((1, tm // MOBA_BLOCK, 128), lambda i: (i, 0, 0)))
    return pl.pallas_call(
        functools.partial(_even_proj_kernel, n_heads=8, qk_dim=qk_dim, scale=scale),
        grid=(nt,),
        in_specs=[row(d)] + [full(a) for a in consts] + [row(128), row(128)],
        out_specs=out_specs, out_shape=out_shape,
        compiler_params=_cp(("parallel",)), name="even_proj",
    )(x, *consts, cos_t, sin_t)


def _odd_proj_kernel(x_ref, ga_ref, wp_ref, bf_ref, gq_ref, gk_ref, bd_ref,
                     q_ref, k_ref, v_ref, lf_ref, kb_ref, vb_ref):
    x = x_ref[...]
    h = _rms(x, ga_ref[...]).astype(BF16)
    y = jnp.dot(h, wp_ref[...], preferred_element_type=F32)
    bd = bd_ref[...]
    q_ref[...] = _head_rms(y[:, 0:1024], gq_ref[...], bd, 1024).astype(BF16)
    k = _head_rms(y[:, 1024:1280], gk_ref[...], bd, 256)
    k_ref[...] = k
    kb_ref[...] = k.astype(BF16)
    v = y[:, 1280:1536]
    v_ref[...] = v
    vb_ref[...] = v.astype(BF16)
    z = y[:, 1536:1664] + bf_ref[...]
    lf_ref[...] = jnp.minimum(z, 0.0) - jnp.log1p(jnp.exp(-jnp.abs(z)))


def _odd_proj(x, ga, wp, bf, gq, gk, bd, *, tm):
    n, d = x.shape
    nt = n // tm
    row = lambda c: pl.BlockSpec((tm, c), lambda i: (i, 0))
    full = lambda a: pl.BlockSpec(a.shape, lambda i: (0,) * a.ndim)
    consts = (ga, wp, bf, gq, gk, bd)
    out_shape = (jax.ShapeDtypeStruct((n, 1024), BF16), jax.ShapeDtypeStruct((n, 256), F32),
                 jax.ShapeDtypeStruct((n, 256), F32), jax.ShapeDtypeStruct((n, 128), F32),
                 jax.ShapeDtypeStruct((n, 256), BF16), jax.ShapeDtypeStruct((n, 256), BF16))
    return pl.pallas_call(
        _odd_proj_kernel, grid=(nt,),
        in_specs=[row(d)] + [full(a) for a in consts],
        out_specs=(row(1024), row(256), row(256), row(128), row(256), row(256)), out_shape=out_shape,
        compiler_params=_cp(("parallel",)), name="odd_proj",
    )(x, *consts)


def _cumsum_kernel(lf_ref, tri_ref, eye_ref, cum_ref, cumt_ref, carry_ref, *, blk):
    carry_ref[...] = jnp.zeros_like(carry_ref)
    s = lf_ref.shape[1]
    tri = tri_ref[...]
    eye = eye_ref[...]
    for i in range(s // blk):
        lf = lf_ref[0, blk * i:blk * (i + 1), :]
        parts = _split(lf, 3)
        cum = sum(jnp.dot(tri, p, preferred_element_type=F32) for p in parts) + carry_ref[...]
        carry_ref[...] = cum[blk - 1:blk, :]
        cum_ref[0, blk * i:blk * (i + 1), :] = cum
        cumt_ref[0, :, blk * i:blk * (i + 1)] = _dot_nt_exact_lhs(eye, cum, 3)


def _prompt_cumsum(lf):
    b, s, w = lf.shape
    blk = 256
    tri = jnp.tril(jnp.ones((blk, blk), F32)).astype(BF16)
    eye = jnp.eye(w, dtype=BF16)
    return pl.pallas_call(
        functools.partial(_cumsum_kernel, blk=blk), grid=(b,),
        in_specs=[pl.BlockSpec((1, s, w), lambda i: (i, 0, 0)),
                  pl.BlockSpec((blk, blk), lambda i: (0, 0)), pl.BlockSpec((w, w), lambda i: (0, 0))],
        out_specs=(pl.BlockSpec((1, s, w), lambda i: (i, 0, 0)), pl.BlockSpec((1, w, s), lambda i: (i, 0, 0))),
        out_shape=(jax.ShapeDtypeStruct((b, s, w), F32), jax.ShapeDtypeStruct((b, w, s), F32)),
        scratch_shapes=[pltpu.VMEM((1, w), F32)],
        compiler_params=_cp(("parallel",)), name="prompt_cumsum",
    )(lf, tri, eye)


def _flash_kernel(*refs, mode, t):
    if mode == "mla":
        q_ref, k_ref, v_ref, o_ref, m_sc, l_sc, acc_sc = refs
    elif mode == "fox":
        q_ref, k_ref, v_ref, cum_ref, cumt_ref, o_ref, m_sc, l_sc, acc_sc = refs
    else:
        q_ref, k_ref, v_ref, kmean_ref, town_ref, tprev_ref, far_ref, o_ref, m_sc, l_sc, acc_sc = refs
    hp = pl.program_id(1)
    qi = pl.program_id(2)
    lane = lax.broadcasted_iota(jnp.int32, (t, LANES), 1)
    if mode == "mla":
        khalf = None
    elif mode == "moba":
        khalf = hp // 2
    else:
        khalf = (hp // 2) % 2

    outs = []
    for a in range(2):
        if mode == "mla":
            q_a = q_ref[:, LANES * a:LANES * (a + 1)]
            ksl = slice(LANES * a, LANES * (a + 1))
        else:
            qb = q_ref[...]
            src = jnp.where(khalf == a, qb, pltpu.roll(qb, 64, 1))
            q_a = jnp.where((lane >= 64).astype(jnp.int32) == khalf, src, jnp.zeros_like(src))
            ksl = slice(0, LANES)
        m_sc[...] = jnp.full(m_sc.shape, NEG, F32)
        l_sc[...] = jnp.zeros(l_sc.shape, F32)
        acc_sc[...] = jnp.zeros(acc_sc.shape, F32)

        if mode == "fox":
            hq = 2 * hp + a
            lane_c = lax.broadcasted_iota(jnp.int32, cum_ref.shape, 1)
            qc = jnp.sum(jnp.where(lane_c == hq, cum_ref[...], 0.0), axis=-1, keepdims=True)
        if mode == "moba":
            km = kmean_ref[0]
            gate = _dot_nt_exact_lhs(q_a, km, 2)
            g = jnp.where(lane < qi, gate, NEG)
            selected = jnp.zeros((t, LANES), jnp.bool_)
            for r in range(MOBA_TOPK):
                mx = jnp.max(g, axis=-1, keepdims=True)
                idx = jnp.min(jnp.where(g == mx, lane, LANES), axis=-1, keepdims=True)
                hit = lane == idx
                selected = jnp.logical_or(selected, jnp.logical_and(hit, r < qi))
                g = jnp.where(hit, -3e38, g)
            selb = jnp.where(selected, 0.0, NEG)
            far = far_ref[0, a]

        def step(j, bias_fn):
            k = k_ref[pl.ds(pl.multiple_of(j * t, t), t), ksl]
            s = _dot_nt(q_a, k)
            s = bias_fn(s, j)
            m_old = m_sc[...]
            m_new = jnp.maximum(m_old, jnp.max(s, axis=-1, keepdims=True))
            alpha = jnp.exp(m_old - m_new)
            p = jnp.exp(s - m_new)
            l_sc[...] = alpha * l_sc[...] + jnp.sum(p, axis=-1, keepdims=True)
            v = v_ref[pl.ds(pl.multiple_of(j * t, t), t), :]
            acc_sc[...] = alpha * acc_sc[...] + jnp.dot(p.astype(BF16), v, preferred_element_type=F32)
            m_sc[...] = m_new

        row_i = lax.broadcasted_iota(jnp.int32, (t, t), 0)
        col_i = lax.broadcasted_iota(jnp.int32, (t, t), 1)
        causal = row_i >= col_i

        if mode == "mla":
            off_bias = lambda s, j: s
            diag_bias = lambda s, j: jnp.where(causal, s, NEG)
            n_off = qi
        elif mode == "fox":
            def kc(j):
                return cumt_ref[0, pl.ds(hq, 1), pl.ds(pl.multiple_of(j * t, t), t)]
            off_bias = lambda s, j: s + (qc - kc(j))
            diag_bias = lambda s, j: jnp.where(causal, s + (qc - kc(j)), NEG)
            n_off = qi
        else:
            def selcol(j):
                return jnp.sum(jnp.where(lane == j, selb, 0.0), axis=-1, keepdims=True)
            off_bias = lambda s, j: s + (selcol(j) + far)
            prev_bias = lambda s, j: s + tprev_ref[a] + selcol(j)
            diag_bias = lambda s, j: s + town_ref[a]
            n_off = jnp.maximum(qi - 1, 0)

        def loop_body(j, carry):
            step(j, off_bias)
            return carry

        lax.fori_loop(0, n_off, loop_body, 0)
        if mode == "moba":
            @pl.when(qi >= 1)
            def _():
                step(qi - 1, prev_bias)
        step(qi, diag_bias)

        o = acc_sc[...] * (1.0 / l_sc[...])
        if mode != "mla":
            o = jnp.where(khalf == a, o, pltpu.roll(o, 64, 1))
        outs.append(o)
    o_ref[...] = jnp.where(lane < 64, outs[0], outs[1]).astype(o_ref.dtype)


def _flash(mode, q, k, v, extra, *, b, s, t):
    nq = s // t
    if mode == "mla":
        n_pairs = v.shape[1] // LANES
        qspec = pl.BlockSpec((t, 2 * LANES), lambda bi, hp, qi: (bi * nq + qi, hp))
        kspec = pl.BlockSpec((s, 2 * LANES), lambda bi, hp, qi: (bi, hp))
        vspec = pl.BlockSpec((s, LANES), lambda bi, hp, qi: (bi, hp))
        extra_specs = []
    else:
        n_pairs = q.shape[1] // LANES
        group_pairs = n_pairs // (k.shape[1] // HEAD_DIM)
        qspec = pl.BlockSpec((t, LANES), lambda bi, hp, qi: (bi * nq + qi, hp))
        kspec = pl.BlockSpec((s, LANES), lambda bi, hp, qi: (bi, hp // (2 * group_pairs)))
        vspec = kspec
        if mode == "fox":
            cum, cumt = extra
            extra_specs = [pl.BlockSpec((None, t, cum.shape[2]), lambda bi, hp, qi: (bi, qi, 0)),
                           pl.BlockSpec((1, cumt.shape[1], s), lambda bi, hp, qi: (bi, 0, 0))]
        else:
            kmean, town, tprev, far = extra
            extra_specs = [pl.BlockSpec((1,) + kmean.shape[1:], lambda bi, hp, qi: (bi, 0, 0)),
                           pl.BlockSpec((2, t, t), lambda bi, hp, qi: (hp, 0, 0)),
                           pl.BlockSpec((2, t, t), lambda bi, hp, qi: (hp, 0, 0)),
                           pl.BlockSpec((1, 2, 1, 1), lambda bi, hp, qi: (hp, 0, 0, 0))]
    return pl.pallas_call(
        functools.partial(_flash_kernel, mode=mode, t=t),
        grid=(b, n_pairs, nq),
        in_specs=[qspec, kspec, vspec] + extra_specs,
        out_specs=pl.BlockSpec((t, LANES), lambda bi, hp, qi: (bi * nq + qi, hp)),
        out_shape=jax.ShapeDtypeStruct((b * s, n_pairs * LANES), BF16),
        scratch_shapes=[pltpu.VMEM((t, 1), F32), pltpu.VMEM((t, 1), F32), pltpu.VMEM((t, LANES), F32)],
        compiler_params=_cp(("parallel", "parallel", "arbitrary")), name="flash_" + mode,
    )(q, k, v, *extra)


def _page_specs(layer, n_chunk_pages, width, order):
    specs = []
    for i in range(n_chunk_pages):
        specs.append(pl.BlockSpec(
            (None, None, 128, width),
            lambda b, j, pt, i=i: (layer, pt[b, order(j) * n_chunk_pages + i], 0, 0)))
    return specs


def _softmax_cols(s_t, m_sc, l_sc):
    m_old = m_sc[...]
    m_new = jnp.maximum(m_old, jnp.max(s_t, axis=0, keepdims=True))
    alpha = jnp.exp(m_old - m_new)
    p_t = jnp.exp(s_t - m_new)
    l_sc[...] = alpha * l_sc[...] + jnp.sum(p_t, axis=0, keepdims=True)
    m_sc[...] = m_new
    return p_t, alpha


def _mla_sample_kernel(pt_ref, *refs, cp, n_chunks, qk_dim, scale):
    c_refs = refs[:cp]
    kr_refs = refs[cp:2 * cp]
    (qbd_ref, qr_ref, cnew_ref, krnew_ref, cosp_ref, sinp_ref, cosn_ref, sinn_ref, wuk_ref, wuv_ref,
     gkr_ref, e64_ref, swap_ref, eye_ref, mask_ref, o_ref, m_sc, l_sc, acc_sc) = refs[2 * cp:]
    j = pl.program_id(1)
    eye = eye_ref[...]

    @pl.when(j == 0)
    def _():
        m_sc[...] = jnp.full(m_sc.shape, NEG, F32)
        l_sc[...] = jnp.zeros(l_sc.shape, F32)
        acc_sc[...] = jnp.zeros(acc_sc.shape, F32)

    def process(c, kr, cos_t, sin_t, mask):
        cb = c.astype(BF16)
        kn = jnp.dot(cb, wuk_ref[...], preferred_element_type=F32)
        ssn = _dot_exact_rhs(kn * kn, e64_ref[...], 2)
        ssr = jnp.sum(kr * kr, axis=-1, keepdims=True)
        inv = lax.rsqrt((ssn + ssr) * (1.0 / qk_dim) + EPS) * scale
        krg = kr * gkr_ref[...]
        krr = krg * cos_t + _dot_exact_rhs(krg, swap_ref[...], 3) * sin_t
        s_t = (_dot(kn, qbd_ref[0]) + _dot(krr, qr_ref[0])) * inv
        if mask is not None:
            s_t = jnp.where(mask, s_t, NEG)
        p_t, alpha = _softmax_cols(s_t, m_sc, l_sc)
        acc_sc[...] = acc_sc[...] * _col_from_row(alpha, eye) + _dot_tn(p_t, cb)

    c = jnp.concatenate([r[...] for r in c_refs], axis=0)
    kr = jnp.concatenate([r[...] for r in kr_refs], axis=0)
    process(c, kr, cosp_ref[...], sinp_ref[...], None)

    @pl.when(j == n_chunks - 1)
    def _():
        process(cnew_ref[0], krnew_ref[0], cosn_ref[...], sinn_ref[...], mask_ref[...] > 0.5)
        pc = acc_sc[...] * _col_from_row(1.0 / l_sc[...], eye)
        n_heads = wuv_ref.shape[0]
        tn = cnew_ref.shape[1]
        for hh in range(n_heads):
            o_ref[0, tn * hh:tn * (hh + 1), :] = _dot(pc[tn * hh:tn * (hh + 1), :], wuv_ref[hh])


def _mla_sample(page_table, cache_c, cache_kr, layer, qbd, qr, c_new, kr_new, cos_p, sin_p, cos_n, sin_n,
                wuk, wuv, gkr, e64, swap, eye, mask, *, qk_dim, scale):
    db, n_pages = page_table.shape
    cp = min(CHUNK_PAGES, n_pages)
    n_chunks = n_pages // cp
    tn = c_new.shape[1]
    n_heads = wuv.shape[0]
    fwd = lambda j: j
    per_seq = lambda a: pl.BlockSpec((1,) + a.shape[1:], lambda b, j, pt: (b,) + (0,) * (a.ndim - 1))
    full = lambda a: pl.BlockSpec(a.shape, lambda b, j, pt: (0,) * a.ndim)
    chunk_rows = lambda a: pl.BlockSpec((cp * 128, a.shape[1]), lambda b, j, pt: (j, 0))
    in_specs = (_page_specs(layer, cp, cache_c.shape[-1], fwd) + _page_specs(layer, cp, cache_kr.shape[-1], fwd)
                + [per_seq(qbd), per_seq(qr), per_seq(c_new), per_seq(kr_new), chunk_rows(cos_p), chunk_rows(sin_p),
                   full(cos_n), full(sin_n), full(wuk), full(wuv), full(gkr), full(e64), full(swap), full(eye), full(mask)])
    grid_spec = pltpu.PrefetchScalarGridSpec(
        num_scalar_prefetch=1, grid=(db, n_chunks), in_specs=in_specs,
        out_specs=pl.BlockSpec((1, n_heads * tn, wuv.shape[2]), lambda b, j, pt: (b, 0, 0)),
        scratch_shapes=[pltpu.VMEM((1, LANES), F32), pltpu.VMEM((1, LANES), F32), pltpu.VMEM((LANES, cache_c.shape[-1]), F32)])
    return pl.pallas_call(
        functools.partial(_mla_sample_kernel, cp=cp, n_chunks=n_chunks, qk_dim=qk_dim, scale=scale),
        grid_spec=grid_spec,
        out_shape=jax.ShapeDtypeStruct((db, n_heads * tn, wuv.shape[2]), F32),
        compiler_params=_cp(("parallel", "arbitrary")), name="mla_sample",
    )(page_table, *([cache_c] * cp), *([cache_kr] * cp), qbd, qr, c_new, kr_new, cos_p, sin_p, cos_n, sin_n,
      wuk, wuv, gkr, e64, swap, eye, mask)


def _moba_sample_kernel(pt_ref, *refs, cp, n_chunks, n_blocks):
    k_refs = refs[:cp]
    v_refs = refs[cp:2 * cp]
    (q_ref, knew_ref, vnew_ref, b31_ref, bown_ref, far_ref, eye_ref, o_ref,
     m_all, l_all, km_all, acc_all) = refs[2 * cp:]
    j = pl.program_id(1)
    q_t = q_ref[0]
    ppb = MOBA_BLOCK // 128
    bpc = cp // ppb

    @pl.when(j == 0)
    def _():
        km_all[...] = jnp.zeros(km_all.shape, F32)
        m_all[...] = jnp.full(m_all.shape, NEG, F32)
        l_all[...] = jnp.zeros(l_all.shape, F32)

    def block(n, k, v, bias):
        s_t = _dot(k, q_t) + bias
        m = jnp.max(s_t, axis=0, keepdims=True)
        p_t = jnp.exp(s_t - m)
        m_all[pl.ds(n, 1), :] = m
        l_all[pl.ds(n, 1), :] = jnp.sum(p_t, axis=0, keepdims=True)
        acc_all[n] = _dot_tn(p_t, v)

    for i in range(bpc):
        n = j * bpc + i
        k = jnp.concatenate([k_refs[ppb * i + u][...] for u in range(ppb)], axis=0)
        v = jnp.concatenate([v_refs[ppb * i + u][...] for u in range(ppb)], axis=0)
        km_all[pl.ds(n, 1), :] = jnp.mean(k, axis=0, keepdims=True)
        is_last = n == n_blocks - 1
        bias = jnp.where(is_last, b31_ref[...], far_ref[...])
        block(n, k, v, bias)

    @pl.when(j == n_chunks - 1)
    def _():
        eye = eye_ref[...]
        block(n_blocks, knew_ref[0], vnew_ref[0], bown_ref[...])
        nb_pad = km_all.shape[0]
        gate = sum(jnp.dot(p, q_t, preferred_element_type=F32) for p in _split(km_all[...], 2))
        blk = lax.broadcasted_iota(jnp.int32, (nb_pad, LANES), 0)
        g = jnp.where(blk < n_blocks, gate, NEG)
        selected = blk == n_blocks
        for r in range(min(MOBA_TOPK, n_blocks + 1)):
            mx = jnp.max(g, axis=0, keepdims=True)
            idx = jnp.min(jnp.where(g == mx, blk, nb_pad), axis=0, keepdims=True)
            hit = blk == idx
            if r < n_blocks:
                selected = jnp.logical_or(selected, hit)
            g = jnp.where(hit, -3e38, g)
        m_sel = jnp.where(selected, m_all[...], NEG)
        m_tot = jnp.max(m_sel, axis=0, keepdims=True)
        w = jnp.where(selected, jnp.exp(m_sel - m_tot), 0.0)
        l_tot = jnp.sum(w * l_all[...], axis=0, keepdims=True)
        w = w * (1.0 / l_tot)
        out = jnp.zeros((LANES, LANES), F32)
        for n in range(n_blocks + 1):
            out = out + _col_from_row(w[n:n + 1, :], eye) * acc_all[n]
        row = lax.broadcasted_iota(jnp.int32, (LANES, LANES), 0)
        lane = lax.broadcasted_iota(jnp.int32, (LANES, LANES), 1)
        rows_per_kv = LANES // 2 // 2
        y = jnp.where(lane // HEAD_DIM == row // rows_per_kv, out, 0.0)
        o_ref[0] = y + pltpu.roll(y, 64, 1)


def _moba_sample(page_table, cache_k, cache_v, layer, q_t, k_new, v_new, b31, bown, far, eye):
    db, n_pages = page_table.shape
    cp = min(CHUNK_PAGES, n_pages)
    n_chunks = n_pages // cp
    n_blocks = n_pages * 128 // MOBA_BLOCK
    nb_pad = -(-(n_blocks + 1) // 8) * 8
    fwd = lambda j: j
    per_seq = lambda a: pl.BlockSpec((1,) + a.shape[1:], lambda b, j, pt: (b,) + (0,) * (a.ndim - 1))
    full = lambda a: pl.BlockSpec(a.shape, lambda b, j, pt: (0,) * a.ndim)
    in_specs = (_page_specs(layer, cp, 128, fwd) + _page_specs(layer, cp, 128, fwd)
                + [per_seq(q_t), per_seq(k_new), per_seq(v_new), full(b31), full(bown), full(far), full(eye)])
    grid_spec = pltpu.PrefetchScalarGridSpec(
        num_scalar_prefetch=1, grid=(db, n_chunks), in_specs=in_specs,
        out_specs=pl.BlockSpec((1, LANES, LANES), lambda b, j, pt: (b, 0, 0)),
        scratch_shapes=[pltpu.VMEM((nb_pad, LANES), F32), pltpu.VMEM((nb_pad, LANES), F32),
                        pltpu.VMEM((nb_pad, LANES), F32), pltpu.VMEM((n_blocks + 1, LANES, LANES), F32)])
    return pl.pallas_call(
        functools.partial(_moba_sample_kernel, cp=cp, n_chunks=n_chunks, n_blocks=n_blocks),
        grid_spec=grid_spec, out_shape=jax.ShapeDtypeStruct((db, LANES, LANES), F32),
        compiler_params=_cp(("parallel", "arbitrary")), name="moba_sample",
    )(page_table, *([cache_k] * cp), *([cache_v] * cp), q_t, k_new, v_new, b31, bown, far, eye)


def _fox_sample_kernel(pt_ref, *refs, cp, n_chunks):
    k_refs = refs[:cp]
    v_refs = refs[cp:2 * cp]
    lf_refs = refs[2 * cp:3 * cp]
    (q_ref, knew_ref, vnew_ref, lfnew_ref, rep_ref, tmask_ref, cmask_ref, eye_ref, o_ref,
     m_sc, l_sc, acc_sc, carry_sc, nrow_sc) = refs[3 * cp:]
    j = pl.program_id(1)
    q_t = q_ref[0]
    eye = eye_ref[...]
    rep = rep_ref[...]

    def attend(s_t, v):
        p_t, alpha = _softmax_cols(s_t, m_sc, l_sc)
        acc_sc[...] = acc_sc[...] * _col_from_row(alpha, eye) + _dot_tn(p_t, v)

    @pl.when(j == 0)
    def _():
        m_sc[...] = jnp.full(m_sc.shape, NEG, F32)
        l_sc[...] = jnp.zeros(l_sc.shape, F32)
        acc_sc[...] = jnp.zeros(acc_sc.shape, F32)
        carry_sc[...] = jnp.zeros(carry_sc.shape, F32)
        lf = lfnew_ref[0]
        tn = lf.shape[0]
        rows = [lf[0:1, :]]
        for t in range(1, tn):
            rows.append(rows[-1] + lf[t:t + 1, :])
        x = _dot_exact_rhs(jnp.concatenate(rows, axis=0), rep, 3)
        nrow = jnp.sum(x * tmask_ref[...], axis=0, keepdims=True)
        nrow_sc[...] = nrow
        s_t = _dot(knew_ref[0], q_t) + (nrow - x)
        attend(jnp.where(cmask_ref[...] > 0.5, s_t, NEG), vnew_ref[0])

    k = jnp.concatenate([r[...] for r in k_refs], axis=0)
    v = jnp.concatenate([r[...] for r in v_refs], axis=0)
    lf = jnp.concatenate([r[...] for r in lf_refs], axis=0)
    tk = k.shape[0]
    e = _dot_exact_rhs(lf, rep, 3)
    x = e
    sh = 1
    row = lax.broadcasted_iota(jnp.int32, (tk, LANES), 0)
    while sh < tk:
        if sh % 8 == 0:
            shifted = jnp.concatenate([x[sh:, :], jnp.zeros((sh, LANES), F32)], axis=0)
        else:
            shifted = jnp.where(row < tk - sh, pltpu.roll(x, tk - sh, 0), 0.0)
        x = x + shifted
        sh *= 2
    carry = carry_sc[...]
    suffix_excl = x - e + carry
    carry_sc[...] = carry + x[0:1, :]
    attend(_dot(k, q_t) + suffix_excl + nrow_sc[...], v)

    @pl.when(j == n_chunks - 1)
    def _():
        out = acc_sc[...] * _col_from_row(1.0 / l_sc[...], eye)
        row_o = lax.broadcasted_iota(jnp.int32, out.shape, 0)
        lane_o = lax.broadcasted_iota(jnp.int32, out.shape, 1)
        rows_per_kv = LANES // (out.shape[1] // HEAD_DIM)
        y = jnp.where(lane_o // HEAD_DIM == row_o // rows_per_kv, out, 0.0)
        z = y[:, :LANES] + y[:, LANES:]
        o_ref[0] = z + pltpu.roll(z, 64, 1)


def _fox_sample(page_table, cache_k, cache_v, cache_lf, layer, q_t, k_new, v_new, lf_new, rep, tmask, cmask, eye):
    db, n_pages = page_table.shape
    cp = min(CHUNK_PAGES, n_pages)
    n_chunks = n_pages // cp
    rev = lambda j: n_chunks - 1 - j
    per_seq = lambda a: pl.BlockSpec((1,) + a.shape[1:], lambda b, j, pt: (b,) + (0,) * (a.ndim - 1))
    full = lambda a: pl.BlockSpec(a.shape, lambda b, j, pt: (0,) * a.ndim)
    kw = cache_k.shape[-1]
    in_specs = (_page_specs(layer, cp, kw, rev) + _page_specs(layer, cp, kw, rev)
                + _page_specs(layer, cp, cache_lf.shape[-1], rev)
                + [per_seq(q_t), per_seq(k_new), per_seq(v_new), per_seq(lf_new), full(rep), full(tmask), full(cmask), full(eye)])
    grid_spec = pltpu.PrefetchScalarGridSpec(
        num_scalar_prefetch=1, grid=(db, n_chunks), in_specs=in_specs,
        out_specs=pl.BlockSpec((1, LANES, LANES), lambda b, j, pt: (b, 0, 0)),
        scratch_shapes=[pltpu.VMEM((1, LANES), F32), pltpu.VMEM((1, LANES), F32), pltpu.VMEM((LANES, kw), F32),
                        pltpu.VMEM((1, LANES), F32), pltpu.VMEM((1, LANES), F32)])
    return pl.pallas_call(
        functools.partial(_fox_sample_kernel, cp=cp, n_chunks=n_chunks),
        grid_spec=grid_spec, out_shape=jax.ShapeDtypeStruct((db, LANES, LANES), F32),
        compiler_params=_cp(("parallel", "arbitrary")), name="fox_sample",
    )(page_table, *([cache_k] * cp), *([cache_v] * cp), *([cache_lf] * cp), q_t, k_new, v_new, lf_new,
      rep, tmask, cmask, eye)


def _mix_mlp_kernel(x_ref, o_ref, wo_ref, g_ref, up_ref, down_ref, y_ref, x1_sc, h_sc, acc_sc):
    f = pl.program_id(1)

    @pl.when(f == 0)
    def _():
        x1 = x_ref[...] + jnp.dot(o_ref[...], wo_ref[...], preferred_element_type=F32)
        x1_sc[...] = x1
        h_sc[...] = _rms(x1, g_ref[...]).astype(BF16)
        acc_sc[...] = jnp.zeros(acc_sc.shape, F32)

    u = jnp.maximum(jnp.dot(h_sc[...], up_ref[...], preferred_element_type=F32), 0.0)
    acc_sc[...] += jnp.dot((u * u).astype(BF16), down_ref[...], preferred_element_type=F32)

    @pl.when(f == pl.num_programs(1) - 1)
    def _():
        y_ref[...] = x1_sc[...] + acc_sc[...]


def _mix_mlp(x, o, wo, g, up, down, *, tm):
    n, d = x.shape
    dff = up.shape[1]
    tf = min(FF_TILE, dff)
    return pl.pallas_call(
        _mix_mlp_kernel, grid=(n // tm, dff // tf),
        in_specs=[pl.BlockSpec((tm, d), lambda i, f: (i, 0)), pl.BlockSpec((tm, o.shape[1]), lambda i, f: (i, 0)),
                  pl.BlockSpec(wo.shape, lambda i, f: (0, 0)), pl.BlockSpec(g.shape, lambda i, f: (0, 0)),
                  pl.BlockSpec((d, tf), lambda i, f: (0, f)), pl.BlockSpec((tf, d), lambda i, f: (f, 0))],
        out_specs=pl.BlockSpec((tm, d), lambda i, f: (i, 0)),
        out_shape=jax.ShapeDtypeStruct((n, d), F32),
        scratch_shapes=[pltpu.VMEM((tm, d), F32), pltpu.VMEM((tm, d), BF16), pltpu.VMEM((tm, d), F32)],
        compiler_params=_cp(("parallel", "arbitrary")), name="mix_mlp",
    )(x, o, wo, g, up, down)


def _t5_bucket(rel):
    n = jnp.maximum(rel, 0)
    exact = REL_BUCKETS // 2
    scaled = jnp.log(jnp.maximum(n, 1).astype(F32) / exact) / math.log(REL_MAX_DIST / exact)
    large = exact + (scaled * (REL_BUCKETS - exact)).astype(jnp.int32)
    return jnp.where(n < exact, n, jnp.minimum(large, REL_BUCKETS - 1))


def _rope_tables(pos, half):
    inv = ROPE_THETA ** (-jnp.arange(half, dtype=F32) / half)
    ang = pos.astype(F32)[:, None] * inv[None, :]
    return jnp.cos(ang), jnp.sin(ang)


def _pick_tile(n, cap):
    t = cap
    while n % t:
        t //= 2
    return t


def kernel(x_prompt, x_sample, cache_mla_latent, cache_mla_krope, cache_moba_k, cache_moba_v, cache_fox_k, cache_fox_v, cache_fox_logf, page_table, rel_bias, g_attn, g_mlp, w_mlp_up, w_mlp_down, w_in_even, g_q_lat, g_kv_lat, w_uq, w_uk, w_uv, g_mla_q, g_mla_k, g_moba_q, g_moba_k, w_o_even, w_in_odd, b_forget, g_fox_q, g_fox_k, w_o_odd):
    B, S, D = x_prompt.shape
    DB, TN, _ = x_sample.shape
    depth = g_attn.shape[0]
    n_pages = page_table.shape[1]
    page = cache_mla_latent.shape[2]
    past = n_pages * page
    n_p, n_s = B * S, DB * TN
    n = n_p + n_s
    n_pool = cache_mla_latent.shape[1]
    q_lora, kv_lora = g_q_lat.shape[1], g_kv_lat.shape[1]
    mla_heads, nope, mla_v = w_uk.shape[2], w_uk.shape[3], w_uv.shape[3]
    qk_dim = w_uq.shape[3]
    rope_d = qk_dim - nope
    half = rope_d // 2
    moba_kv = cache_moba_k.shape[3]
    moba_heads = rel_bias.shape[1]
    moba_g = moba_heads // moba_kv
    fox_heads = b_forget.shape[1]
    fox_kv = cache_fox_k.shape[3]
    fox_g = fox_heads // fox_kv
    mla_scale = float(qk_dim) ** -0.5
    att_scale = HEAD_DIM ** -0.5
    assert (q_lora, kv_lora, mla_heads, nope, rope_d, mla_v) == (256, 256, 8, 64, 32, 64)
    assert (moba_heads, moba_kv, fox_heads, fox_kv, page, TN) == (8, 2, 16, 4, 128, 8)
    assert past % MOBA_BLOCK == 0 and S % 512 == 0 and n_p % 256 == 0 and n_s % 256 == 0
    tm = _pick_tile(math.gcd(n_p, n_s), TOKEN_TILE)

    pos = jnp.concatenate([jnp.tile(jnp.arange(S, dtype=jnp.int32), B),
                           jnp.tile(past + jnp.arange(TN, dtype=jnp.int32), DB)])
    cos, sin = _rope_tables(pos, half)
    ones64 = jnp.ones((n, 64), F32)
    zeros32 = jnp.zeros((n, 32), F32)
    cos_t = jnp.concatenate([ones64, cos, cos, zeros32], axis=1)
    sin_t = jnp.concatenate([jnp.zeros((n, 64), F32), -sin, sin, zeros32], axis=1)
    kpos = jnp.arange(past + TN, dtype=jnp.int32)
    kcos, ksin = _rope_tables(kpos, half)
    kcos32 = jnp.concatenate([kcos, kcos], axis=1)
    ksin32 = jnp.concatenate([-ksin, ksin], axis=1)
    swap = jnp.roll(jnp.eye(rope_d, dtype=F32), half, axis=1).astype(BF16)

    bd = (jnp.arange(512)[:, None] // HEAD_DIM == jnp.arange(512)[None, :] // HEAD_DIM).astype(BF16)
    eye128 = jnp.eye(LANES, dtype=F32)
    r_idx = jnp.arange(LANES)
    e64 = ((jnp.arange(mla_heads * nope)[:, None] // nope == r_idx[None, :] // TN) & (r_idx[None, :] < mla_heads * TN)).astype(BF16)
    mla_mask = ((jnp.arange(TN)[:, None] <= r_idx[None, :] % TN)).astype(F32)
    rep_fox = (jnp.arange(cache_fox_logf.shape[-1])[:, None] == r_idx[None, :] // TN).astype(BF16)
    tmask = (jnp.arange(TN)[:, None] == r_idx[None, :] % TN).astype(F32)

    rb = rel_bias.astype(F32)
    ii = jnp.arange(MOBA_BLOCK)
    rel_own = ii[:, None] - ii[None, :]
    town = jnp.where(rel_own[None] >= 0, rb[_t5_bucket(rel_own)].transpose(2, 0, 1), NEG)
    tprev = rb[_t5_bucket(rel_own + MOBA_BLOCK)].transpose(2, 0, 1)
    assert MOBA_BLOCK + 1 >= REL_MAX_DIST
    far_h = rb[REL_BUCKETS - 1]
    far_p = far_h.reshape(moba_heads // 2, 2, 1, 1)
    head_of_row = jnp.where(r_idx < moba_heads * TN, r_idx // TN, 0)
    t_of_row = r_idx % TN
    real_row = (r_idx < moba_heads * TN)
    rel31 = MOBA_BLOCK + t_of_row[None, :] - ii[:, None]
    b31 = jnp.where(real_row[None, :], rb[_t5_bucket(rel31), head_of_row[None, :]], 0.0)
    relown_s = t_of_row[None, :] - jnp.arange(TN)[:, None]
    bown = jnp.where(relown_s >= 0, jnp.where(real_row[None, :], rb[_t5_bucket(relown_s), head_of_row[None, :]], 0.0), NEG)
    far_s = jnp.where(real_row, far_h[head_of_row], 0.0)[None, :]

    pt = page_table.astype(jnp.int32)
    x = jnp.concatenate([x_prompt.reshape(n_p, D), x_sample.reshape(n_s, D)], axis=0)
    outs = {k: [] for k in ("lat", "kr", "mk", "mv", "fk", "fv", "lf")}

    for layer in range(depth):
        ga = g_attn[layer][None, :]
        if layer % 2 == 0:
            e = layer // 2
            w = w_in_even[e]
            wp = jnp.concatenate([w[:, 0:512], w[:, 544:1312], jnp.zeros((D, nope), F32), w[:, 512:544],
                                  jnp.zeros((D, LANES - qk_dim), F32)], axis=1).astype(BF16)
            wuq = jnp.pad(w_uq[e], ((0, 0), (0, 0), (0, LANES - qk_dim))).reshape(q_lora, mla_heads * LANES).astype(BF16)
            wuk_pad = jnp.pad(w_uk[e], ((0, 0), (0, 0), (0, LANES - nope))).reshape(kv_lora, mla_heads * LANES).astype(BF16)
            wuk = w_uk[e].reshape(kv_lora, mla_heads * nope).astype(BF16)
            wuv = w_uv[e].reshape(kv_lora, mla_heads * mla_v).astype(BF16)
            wuv_h = w_uv[e].transpose(1, 0, 2).astype(BF16)
            gq = jnp.pad(g_mla_q[e], (0, LANES - qk_dim))[None, :]
            gkn = jnp.concatenate([g_mla_k[e][:nope], jnp.ones((LANES - nope,), F32)])[None, :]
            gkr = jnp.concatenate([jnp.zeros((nope,), F32), g_mla_k[e][nope:], jnp.zeros((LANES - qk_dim,), F32)])[None, :]
            gmq = (jnp.tile(g_moba_q[e], moba_heads) * att_scale)[None, :]
            gmk = jnp.tile(g_moba_k[e], moba_kv)[None, :]
            (c, krb, mk, mv, qmla, kmla, vmla, qmo, kmo, vmo, kmean) = _even_proj(
                x, ga, wp, g_q_lat[e][None, :], g_kv_lat[e][None, :], wuq, gq, gkn, gkr, wuk_pad, wuv, gmq, gmk, bd,
                cos_t, sin_t, tm=tm, qk_dim=qk_dim, scale=mla_scale)
            kr = krb[:, 64:64 + rope_d]
            o1_p = _flash("mla", qmla, kmla, vmla, (), b=B, s=S, t=512)
            nblk = S // MOBA_BLOCK
            km_p = kmean.reshape(n // MOBA_BLOCK, 128)[:B * nblk].reshape(B, nblk, 128)
            km_p = jnp.pad(km_p, ((0, 0), (0, LANES - nblk), (0, 0)))
            o2_p = _flash("moba", qmo, kmo, vmo, (km_p, town, tprev, far_p), b=B, s=S, t=MOBA_BLOCK)
            qs = qmla[n_p:].reshape(DB, TN, mla_heads, LANES)
            qn_t = qs[..., :nope].transpose(0, 2, 3, 1)
            qbd = jnp.einsum('bhdt,hj->bhdjt', qn_t, jnp.eye(mla_heads, dtype=BF16)).reshape(DB, mla_heads * nope, mla_heads * TN)
            qbd = jnp.pad(qbd, ((0, 0), (0, 0), (0, LANES - mla_heads * TN)))
            qr_t = qs[..., nope:qk_dim].transpose(0, 3, 2, 1).reshape(DB, rope_d, mla_heads * TN)
            qr_t = jnp.pad(qr_t, ((0, 0), (0, 0), (0, LANES - mla_heads * TN)))
            o1_s = _mla_sample(pt, cache_mla_latent, cache_mla_krope, e, qbd, qr_t,
                               c[n_p:].reshape(DB, TN, kv_lora), kr[n_p:].reshape(DB, TN, rope_d),
                               kcos32[:past], ksin32[:past], kcos32[past:], ksin32[past:],
                               wuk, wuv_h, g_mla_k[e][nope:][None, :], e64, swap, eye128, mla_mask,
                               qk_dim=qk_dim, scale=mla_scale)
            o1_s = o1_s.reshape(DB, mla_heads, TN, mla_v).transpose(0, 2, 1, 3).reshape(n_s, mla_heads * mla_v)
            q2 = qmo[n_p:].reshape(DB, TN, moba_kv, moba_g, HEAD_DIM).transpose(0, 2, 4, 3, 1)
            q2t = jnp.einsum('bkdgt,kj->bkdjgt', q2, jnp.eye(moba_kv, dtype=BF16)).reshape(DB, moba_kv * HEAD_DIM, moba_heads * TN)
            q2t = jnp.pad(q2t, ((0, 0), (0, 0), (0, LANES - moba_heads * TN)))
            o2_s = _moba_sample(pt, cache_moba_k.reshape(cache_moba_k.shape[:3] + (-1,)),
                                cache_moba_v.reshape(cache_moba_v.shape[:3] + (-1,)), e, q2t,
                                mk[n_p:].reshape(DB, TN, -1), mv[n_p:].reshape(DB, TN, -1), b31, bown, far_s, eye128)
            o2_s = o2_s[:, :moba_heads * TN, :HEAD_DIM].reshape(DB, moba_heads, TN, HEAD_DIM).transpose(0, 2, 1, 3)
            o2_s = o2_s.reshape(n_s, moba_heads * HEAD_DIM)
            o = jnp.concatenate([jnp.concatenate([o1_p, o2_p], axis=1),
                                 jnp.concatenate([o1_s.astype(BF16), o2_s.astype(BF16)], axis=1)], axis=0)
            wo = w_o_even[e].astype(BF16)
            outs["lat"].append(c); outs["kr"].append(kr); outs["mk"].append(mk); outs["mv"].append(mv)
        else:
            jl = layer // 2
            w = w_in_odd[jl]
            wp = jnp.concatenate([w, jnp.zeros((D, LANES - fox_heads), F32)], axis=1).astype(BF16)
            bf = jnp.pad(b_forget[jl], (0, LANES - fox_heads))[None, :]
            gq = (jnp.tile(g_fox_q[jl], fox_heads) * att_scale)[None, :]
            gk = jnp.tile(g_fox_k[jl], fox_kv)[None, :]
            q, k, v, lf, kb, vb = _odd_proj(x, ga, wp, bf, gq, gk, bd, tm=tm)
            cum, cumt = _prompt_cumsum(lf[:n_p].reshape(B, S, LANES))
            o_p = _flash("fox", q, kb, vb, (cum, cumt), b=B, s=S, t=512)
            qf = q[n_p:].reshape(DB, TN, fox_kv, fox_g, HEAD_DIM).transpose(0, 2, 4, 3, 1)
            qft = jnp.einsum('bkdgt,kj->bkdjgt', qf, jnp.eye(fox_kv, dtype=BF16)).reshape(DB, fox_kv * HEAD_DIM, fox_heads * TN)
            lfh = lf[:, :fox_heads]
            cmask = mla_mask
            o_s = _fox_sample(pt, cache_fox_k.reshape(cache_fox_k.shape[:3] + (-1,)),
                              cache_fox_v.reshape(cache_fox_v.shape[:3] + (-1,)), cache_fox_logf, jl, qft,
                              k[n_p:].reshape(DB, TN, -1), v[n_p:].reshape(DB, TN, -1), lfh[n_p:].reshape(DB, TN, fox_heads),
                              rep_fox, tmask, cmask, eye128)
            o_s = o_s[:, :, :HEAD_DIM].reshape(DB, fox_heads, TN, HEAD_DIM).transpose(0, 2, 1, 3).reshape(n_s, fox_heads * HEAD_DIM)
            o = jnp.concatenate([o_p, o_s.astype(BF16)], axis=0)
            wo = w_o_odd[jl].astype(BF16)
            outs["fk"].append(k); outs["fv"].append(v); outs["lf"].append(lfh)
        x = _mix_mlp(x, o, wo, g_mlp[layer][None, :], w_mlp_up[layer].astype(BF16), w_mlp_down[layer].astype(BF16), tm=tm)

    def split(name, tail):
        a = jnp.stack(outs[name])
        return a[:, :n_p].reshape((a.shape[0], B, S) + tail), a[:, n_p:].reshape((a.shape[0], DB, TN) + tail)

    lat_p, lat_s = split("lat", (kv_lora,))
    kr_p, kr_s = split("kr", (rope_d,))
    mk_p, mk_s = split("mk", (moba_kv, HEAD_DIM))
    mv_p, mv_s = split("mv", (moba_kv, HEAD_DIM))
    fk_p, fk_s = split("fk", (fox_kv, HEAD_DIM))
    fv_p, fv_s = split("fv", (fox_kv, HEAD_DIM))
    lf_p, lf_s = split("lf", (fox_heads,))
    return (x[:n_p].reshape(B, S, D), x[n_p:].reshape(DB, TN, D), lat_p, lat_s, kr_p, kr_s, mk_p, mk_s, mv_p, mv_s,
            fk_p, fk_s, fv_p, fv_s, lf_p, lf_s)
```

```python
import functools
import math

import jax
import jax.numpy as jnp
from jax import lax
from jax.experimental import pallas as pl
from jax.experimental.pallas import tpu as pltpu

F32 = jnp.float32
BF16 = jnp.bfloat16
NEG = -1e30
EPS = 1e-6
ROPE_THETA = 10000.0
HEAD_DIM = 64
MOBA_BLOCK = 256
MOBA_TOPK = 3
REL_BUCKETS = 32
REL_MAX_DIST = 128
LANES = 128
TOKEN_TILE = 512
FF_TILE = 1024
CHUNK_PAGES = 16
VMEM_LIMIT = 56 * 1024 * 1024


def _cp(sem, vmem=VMEM_LIMIT):
    return pltpu.CompilerParams(dimension_semantics=sem, vmem_limit_bytes=vmem)


def _dot(a, b):
    return jnp.dot(a.astype(BF16), b.astype(BF16), preferred_element_type=F32)


def _dot_nt(a, b):
    return lax.dot_general(a.astype(BF16), b.astype(BF16), (((1,), (1,)), ((), ())),
                           preferred_element_type=F32)


def _split(a, n):
    parts = []
    r = a.astype(F32)
    for _ in range(n):
        p = r.astype(BF16)
        parts.append(p)
        r = r - p.astype(F32)
    return parts


def _dot_exact_rhs(a, b, n=3):
    return sum(jnp.dot(p, b, preferred_element_type=F32) for p in _split(a, n))


def _dot_exact_lhs(a, b, n=3):
    return sum(jnp.dot(a, p, preferred_element_type=F32) for p in _split(b, n))


def _dot_nt_exact_lhs(a, b, n=3):
    return sum(lax.dot_general(a, p, (((1,), (1,)), ((), ())), preferred_element_type=F32)
               for p in _split(b, n))


def _rms(v, g):
    return v * lax.rsqrt(jnp.mean(v * v, axis=-1, keepdims=True) + EPS) * g


def _head_rms(v, g, bd, width):
    step = min(width, bd.shape[0])
    outs = []
    for s in range(0, width, step):
        blk = v[:, s:s + step]
        ss = _dot_exact_rhs(blk * blk, bd[:step, :step], 2)
        outs.append(blk * lax.rsqrt(ss * (1.0 / HEAD_DIM) + EPS))
    out = outs[0] if len(outs) == 1 else jnp.concatenate(outs, axis=-1)
    return out * g


def _rot_block(v, cos_t, sin_t):
    lane = lax.broadcasted_iota(jnp.int32, v.shape, 1)
    partner = jnp.where(lane < 80, pltpu.roll(v, LANES - 16, 1), pltpu.roll(v, 16, 1))
    return v * cos_t + partner * sin_t


def _repeat_rows(x, reps):
    return jnp.concatenate([jnp.broadcast_to(x[i:i + 1, :], (reps, x.shape[1])) for i in range(x.shape[0])], axis=0)


def _top_blocks(gate, lane, n_past, n_pick):
    g = jnp.where(lane < n_past, gate, NEG)
    selected = jnp.zeros(gate.shape, jnp.bool_)
    for r in range(n_pick):
        mx = jnp.max(g, axis=-1, keepdims=True)
        idx = jnp.min(jnp.where(g == mx, lane, LANES), axis=-1, keepdims=True)
        hit = lane == idx
        selected = jnp.logical_or(selected, jnp.logical_and(hit, r < n_past))
        g = jnp.where(hit, -3e38, g)
    return selected


def _even_proj_kernel(x_ref, ga_ref, wp_ref, gql_ref, gkv_ref, wuq_ref, gq_ref, gkn_ref, gkr_ref,
                      wuk_ref, wuv_ref, gmq_ref, gmk_ref, bd_ref, cos_ref, sin_ref,
                      c_ref, krb_ref, mk_ref, mv_ref, qmla_ref, kmla_ref, vmla_ref,
                      qmo_ref, kmo_ref, vmo_ref, kmean_ref, *, n_heads, qk_dim, scale):
    x = x_ref[...]
    h = _rms(x, ga_ref[...]).astype(BF16)
    y = jnp.dot(h, wp_ref[...], preferred_element_type=F32)
    cos_t = cos_ref[...]
    sin_t = sin_ref[...]
    bd = bd_ref[...]

    c = _rms(y[:, 256:512], gkv_ref[...])
    c_ref[...] = c
    cb = c.astype(BF16)

    ql = _rms(y[:, 0:256], gql_ref[...]).astype(BF16)
    q = jnp.dot(ql, wuq_ref[...], preferred_element_type=F32)
    gq = gq_ref[...]
    gkn = gkn_ref[...]
    inv_d = 1.0 / qk_dim
    for hh in range(n_heads):
        sl = slice(LANES * hh, LANES * (hh + 1))
        qh = q[:, sl]
        qh = qh * lax.rsqrt(jnp.sum(qh * qh, axis=-1, keepdims=True) * inv_d + EPS) * gq
        qmla_ref[:, sl] = _rot_block(qh * gkn, cos_t, sin_t).astype(BF16)

    krb = y[:, 1280:1408]
    krb_ref[...] = krb
    ssr = jnp.sum(krb * krb, axis=-1, keepdims=True)
    krr = _rot_block(krb * gkr_ref[...], cos_t, sin_t)
    kn = jnp.dot(cb, wuk_ref[...], preferred_element_type=F32)
    for hh in range(n_heads):
        sl = slice(LANES * hh, LANES * (hh + 1))
        kh = kn[:, sl]
        inv = lax.rsqrt((jnp.sum(kh * kh, axis=-1, keepdims=True) + ssr) * inv_d + EPS) * scale
        kmla_ref[:, sl] = ((kh + krr) * inv).astype(BF16)
    vmla_ref[...] = jnp.dot(cb, wuv_ref[...], preferred_element_type=F32).astype(BF16)

    mq = _head_rms(y[:, 512:1024], gmq_ref[...], bd, 512)
    qmo_ref[...] = mq.astype(BF16)
    mk = _head_rms(y[:, 1024:1152], gmk_ref[...], bd, 128)
    mk_ref[...] = mk
    kmo_ref[...] = mk.astype(BF16)
    mv = y[:, 1152:1280]
    mv_ref[...] = mv
    vmo_ref[...] = mv.astype(BF16)
    tm = x.shape[0]
    for i in range(tm // MOBA_BLOCK):
        kmean_ref[0, i:i + 1, :] = jnp.mean(mk[MOBA_BLOCK * i:MOBA_BLOCK * (i + 1)], axis=0, keepdims=True)


def _even_proj(x, ga, wp, gql, gkv, wuq, gq, gkn, gkr, wuk, wuv, gmq, gmk, bd, cos_t, sin_t, *, tm, qk_dim, scale):
    n, d = x.shape
    nt = n // tm
    row = lambda c: pl.BlockSpec((tm, c), lambda i: (i, 0))
    full = lambda a: pl.BlockSpec(a.shape, lambda i: (0,) * a.ndim)
    consts = (ga, wp, gql, gkv, wuq, gq, gkn, gkr, wuk, wuv, gmq, gmk, bd)
    out_shape = (
        jax.ShapeDtypeStruct((n, 256), F32), jax.ShapeDtypeStruct((n, 128), F32),
        jax.ShapeDtypeStruct((n, 128), F32), jax.ShapeDtypeStruct((n, 128), F32),
        jax.ShapeDtypeStruct((n, 1024), BF16), jax.ShapeDtypeStruct((n, 1024), BF16),
        jax.ShapeDtypeStruct((n, 512), BF16), jax.ShapeDtypeStruct((n, 512), BF16),
        jax.ShapeDtypeStruct((n, 128), BF16), jax.ShapeDtypeStruct((n, 128), BF16),
        jax.ShapeDtypeStruct((nt, tm // MOBA_BLOCK, 128), F32),
    )
    out_specs = (row(256), row(128), row(128), row(128), row(1024), row(1024), row(512), row(512),
                 row(128), row(128), pl.BlockSpec((1, tm // MOBA_BLOCK, 128), lambda i: (i, 0, 0)))
    return pl.pallas_call(
        functools.partial(_even_proj_kernel, n_heads=8, qk_dim=qk_dim, scale=scale),
        grid=(nt,),
        in_specs=[row(d)] + [full(a) for a in consts] + [row(128), row(128)],
        out_specs=out_specs, out_shape=out_shape,
        compiler_params=_cp(("parallel",)), name="even_proj",
    )(x, *consts, cos_t, sin_t)


def _odd_proj_kernel(x_ref, ga_ref, wp_ref, bf_ref, gq_ref, gk_ref, bd_ref,
                     q_ref, k_ref, v_ref, lf_ref, kb_ref, vb_ref):
    x = x_ref[...]
    h = _rms(x, ga_ref[...]).astype(BF16)
    y = jnp.dot(h, wp_ref[...], preferred_element_type=F32)
    bd = bd_ref[...]
    q_ref[...] = _head_rms(y[:, 0:1024], gq_ref[...], bd, 1024).astype(BF16)
    k = _head_rms(y[:, 1024:1280], gk_ref[...], bd, 256)
    k_ref[...] = k
    kb_ref[...] = k.astype(BF16)
    v = y[:, 1280:1536]
    v_ref[...] = v
    vb_ref[...] = v.astype(BF16)
    z = y[:, 1536:1664] + bf_ref[...]
    lf_ref[...] = jnp.minimum(z, 0.0) - jnp.log1p(jnp.exp(-jnp.abs(z)))


def _odd_proj(x, ga, wp, bf, gq, gk, bd, *, tm):
    n, d = x.shape
    nt = n // tm
    row = lambda c: pl.BlockSpec((tm, c), lambda i: (i, 0))
    full = lambda a: pl.BlockSpec(a.shape, lambda i: (0,) * a.ndim)
    consts = (ga, wp, bf, gq, gk, bd)
    out_shape = (jax.ShapeDtypeStruct((n, 1024), BF16), jax.ShapeDtypeStruct((n, 256), F32),
                 jax.ShapeDtypeStruct((n, 256), F32), jax.ShapeDtypeStruct((n, 128), F32),
                 jax.ShapeDtypeStruct((n, 256), BF16), jax.ShapeDtypeStruct((n, 256), BF16))
    return pl.pallas_call(
        _odd_proj_kernel, grid=(nt,),
        in_specs=[row(d)] + [full(a) for a in consts],
        out_specs=(row(1024), row(256), row(256), row(128), row(256), row(256)), out_shape=out_shape,
        compiler_params=_cp(("parallel",)), name="odd_proj",
    )(x, *consts)


def _cumsum_kernel(lf_ref, tri_ref, eye_ref, cum_ref, cumt_ref, carry_ref, *, blk):
    carry_ref[...] = jnp.zeros_like(carry_ref)
    s = lf_ref.shape[1]
    tri = tri_ref[...]
    eye = eye_ref[...]
    for i in range(s // blk):
        lf = lf_ref[0, blk * i:blk * (i + 1), :]
        cum = _dot_exact_lhs(tri, lf, 3) + carry_ref[...]
        carry_ref[...] = cum[blk - 1:blk, :]
        cum_ref[0, blk * i:blk * (i + 1), :] = cum
        cumt_ref[0, :, blk * i:blk * (i + 1)] = _dot_nt_exact_lhs(eye, cum, 3)


def _prompt_cumsum(lf):
    b, s, w = lf.shape
    blk = 256
    tri = jnp.tril(jnp.ones((blk, blk), F32)).astype(BF16)
    eye = jnp.eye(w, dtype=BF16)
    return pl.pallas_call(
        functools.partial(_cumsum_kernel, blk=blk), grid=(b,),
        in_specs=[pl.BlockSpec((1, s, w), lambda i: (i, 0, 0)),
                  pl.BlockSpec((blk, blk), lambda i: (0, 0)), pl.BlockSpec((w, w), lambda i: (0, 0))],
        out_specs=(pl.BlockSpec((1, s, w), lambda i: (i, 0, 0)), pl.BlockSpec((1, w, s), lambda i: (i, 0, 0))),
        out_shape=(jax.ShapeDtypeStruct((b, s, w), F32), jax.ShapeDtypeStruct((b, w, s), F32)),
        scratch_shapes=[pltpu.VMEM((1, w), F32)],
        compiler_params=_cp(("parallel",)), name="prompt_cumsum",
    )(lf, tri, eye)


def _bias_tables_kernel(rb_ref, bown_ref, bprev_ref, b31_ref, bnew_ref,
                        town_ref, tprev_ref, t31_ref, tnew_ref, far_ref):
    h = pl.program_id(0)

    def lookup(bucket):
        out = jnp.full(bucket.shape, NEG, F32)
        for b in range(REL_BUCKETS):
            out = jnp.where(bucket == b, rb_ref[b, h], out)
        return out

    town_ref[0] = lookup(bown_ref[...])
    tprev_ref[0] = lookup(bprev_ref[...])
    t31_ref[0] = lookup(b31_ref[...])
    tnew_ref[0] = lookup(bnew_ref[...])
    far_ref[0] = jnp.full(far_ref.shape[1:], rb_ref[REL_BUCKETS - 1, h], F32)


def _bias_tables(rel_bias, bown, bprev, b31, bnew):
    n_heads = rel_bias.shape[1]
    full = lambda a: pl.BlockSpec(a.shape, lambda h: (0,) * a.ndim)
    per_head = lambda a: pl.BlockSpec((1,) + a.shape, lambda h: (h,) + (0,) * a.ndim)
    shp = lambda a: jax.ShapeDtypeStruct((n_heads,) + a.shape, F32)
    far = jax.ShapeDtypeStruct((n_heads, 8, LANES), F32)
    return pl.pallas_call(
        _bias_tables_kernel, grid=(n_heads,),
        in_specs=[pl.BlockSpec(memory_space=pltpu.SMEM), full(bown), full(bprev), full(b31), full(bnew)],
        out_specs=(per_head(bown), per_head(bprev), per_head(b31), per_head(bnew),
                   pl.BlockSpec((1, 8, LANES), lambda h: (h, 0, 0))),
        out_shape=(shp(bown), shp(bprev), shp(b31), shp(bnew), far),
        compiler_params=_cp(("parallel",)), name="bias_tables",
    )(rel_bias, bown, bprev, b31, bnew)


def _flash_kernel(*refs, mode, t):
    if mode == "mla":
        q_ref, k_ref, v_ref, o_ref, m_sc, acc_sc = refs
    elif mode == "fox":
        q_ref, k_ref, v_ref, cum_ref, cumt_ref, o_ref, m_sc, acc_sc = refs
    else:
        q_ref, k_ref, v_ref, kmean_ref, town_ref, tprev_ref, far_ref, o_ref, m_sc, acc_sc = refs
    hp = pl.program_id(1)
    qi = pl.program_id(2)
    lane = lax.broadcasted_iota(jnp.int32, (t, LANES), 1)
    upper = lane >= 64
    if mode == "mla":
        vhalf = (0, 1)
    else:
        khalf = (hp // 2) % 2
        vhalf = (khalf, khalf)

    q = []
    for a in range(2):
        if mode == "mla":
            q.append(q_ref[:, LANES * a:LANES * (a + 1)])
        else:
            qb = q_ref[...]
            src = jnp.where(khalf == a, qb, pltpu.roll(qb, 64, 1))
            q.append(jnp.where(upper.astype(jnp.int32) == khalf, src, jnp.zeros_like(src)))
    m_sc[...] = jnp.full(m_sc.shape, NEG, F32)
    acc_sc[...] = jnp.zeros(acc_sc.shape, F32)

    if mode == "fox":
        lane_c = lax.broadcasted_iota(jnp.int32, cum_ref.shape, 1)
        qc = [jnp.sum(jnp.where(lane_c == 2 * hp + a, cum_ref[...], 0.0), axis=-1, keepdims=True) for a in range(2)]
    if mode == "moba":
        km = kmean_ref[0]
        selb = []
        for a in range(2):
            gate = _dot_nt_exact_lhs(q[a], km, 2)
            selected = _top_blocks(gate, lane, qi, MOBA_TOPK)
            selb.append(jnp.where(selected, 0.0, NEG).astype(BF16))
        row_b = lax.broadcasted_iota(jnp.int32, (LANES, LANES), 0)

    row_i = lax.broadcasted_iota(jnp.int32, (t, t), 0)
    col_i = lax.broadcasted_iota(jnp.int32, (t, t), 1)
    causal = row_i >= col_i

    def step(j, kind):
        rows = pl.ds(pl.multiple_of(j * t, t), t)
        v = v_ref[rows, :]
        ones = jnp.ones_like(v)
        if mode == "moba":
            onehot = (row_b == j).astype(BF16)
        for a in range(2):
            k = k_ref[rows, LANES * a:LANES * (a + 1)] if mode == "mla" else k_ref[rows, :]
            s = _dot_nt(q[a], k)
            if mode == "fox":
                kc = cumt_ref[0, pl.ds(2 * hp + a, 1), rows]
                s = (s - kc) + qc[a]
            if mode == "moba":
                if kind == "diag":
                    s = s + town_ref[a]
                else:
                    sel = jnp.dot(selb[a], onehot, preferred_element_type=F32)
                    if kind == "prev":
                        s = s + tprev_ref[a] + jnp.concatenate([sel] * (t // LANES), axis=1)
                    else:
                        sel = sel + far_ref[a, 0:1, :]
                        s = s + jnp.concatenate([sel] * (t // LANES), axis=1)
            elif kind == "diag":
                s = jnp.where(causal, s, NEG)
            m_old = m_sc[a]
            m_new = jnp.maximum(m_old, jnp.max(s, axis=-1, keepdims=True))
            alpha = jnp.exp(m_old - m_new)
            p = jnp.exp(s - m_new)
            if mode == "mla":
                v_a = jnp.where(upper, v, ones) if a else jnp.where(upper, ones, v)
            else:
                v_a = jnp.where(upper.astype(jnp.int32) == khalf, v, ones)
            acc_sc[a] = alpha * acc_sc[a] + jnp.dot(p.astype(BF16), v_a, preferred_element_type=F32)
            m_sc[a] = m_new

    def loop_body(j, carry):
        step(j, "off")
        return carry

    if mode == "moba":
        lax.fori_loop(0, jnp.maximum(qi - 1, 0), loop_body, 0)

        @pl.when(qi >= 1)
        def _():
            step(qi - 1, "prev")
    else:
        lax.fori_loop(0, qi, loop_body, 0)
    step(qi, "diag")

    outs = []
    for a in range(2):
        acc = acc_sc[a]
        o = acc * (1.0 / pltpu.roll(acc, 64, 1))
        if mode != "mla":
            o = jnp.where(khalf == a, o, pltpu.roll(o, 64, 1))
        outs.append(o)
    o_ref[...] = jnp.where(upper, outs[1], outs[0]).astype(o_ref.dtype)


def _flash(mode, q, k, v, extra, *, b, s, t):
    nq = s // t
    if mode == "mla":
        n_pairs = v.shape[1] // LANES
        qspec = pl.BlockSpec((t, 2 * LANES), lambda bi, hp, qi: (bi * nq + qi, hp))
        kspec = pl.BlockSpec((s, 2 * LANES), lambda bi, hp, qi: (bi, hp))
        vspec = pl.BlockSpec((s, LANES), lambda bi, hp, qi: (bi, hp))
        extra_specs = []
    else:
        n_pairs = q.shape[1] // LANES
        group_pairs = n_pairs // (k.shape[1] // HEAD_DIM)
        qspec = pl.BlockSpec((t, LANES), lambda bi, hp, qi: (bi * nq + qi, hp))
        kspec = pl.BlockSpec((s, LANES), lambda bi, hp, qi: (bi, hp // (2 * group_pairs)))
        vspec = kspec
        if mode == "fox":
            cum, cumt = extra
            extra_specs = [pl.BlockSpec((None, t, cum.shape[2]), lambda bi, hp, qi: (bi, qi, 0)),
                           pl.BlockSpec((1, cumt.shape[1], s), lambda bi, hp, qi: (bi, 0, 0))]
        else:
            kmean, town, tprev, far = extra
            extra_specs = [pl.BlockSpec((1,) + kmean.shape[1:], lambda bi, hp, qi: (bi, 0, 0)),
                           pl.BlockSpec((2, t, t), lambda bi, hp, qi: (hp, 0, 0)),
                           pl.BlockSpec((2, t, t), lambda bi, hp, qi: (hp, 0, 0)),
                           pl.BlockSpec((2,) + far.shape[1:], lambda bi, hp, qi: (hp, 0, 0))]
    return pl.pallas_call(
        functools.partial(_flash_kernel, mode=mode, t=t),
        grid=(b, n_pairs, nq),
        in_specs=[qspec, kspec, vspec] + extra_specs,
        out_specs=pl.BlockSpec((t, LANES), lambda bi, hp, qi: (bi * nq + qi, hp)),
        out_shape=jax.ShapeDtypeStruct((b * s, n_pairs * LANES), BF16),
        scratch_shapes=[pltpu.VMEM((2, t, 1), F32), pltpu.VMEM((2, t, LANES), F32)],
        compiler_params=_cp(("parallel", "parallel", "arbitrary")), name="flash_" + mode,
    )(q, k, v, *extra)


def _page_specs(layer, n_chunk_pages, shape, order):
    specs = []
    for i in range(n_chunk_pages):
        specs.append(pl.BlockSpec(
            (None, None) + shape,
            lambda b, j, pt, i=i: (layer, pt[b, order(j) * n_chunk_pages + i], 0, 0)))
    return specs


def _online_softmax(s, m_sc, l_sc, acc_sc, pv):
    m_old = m_sc[...]
    m_new = jnp.maximum(m_old, jnp.max(s, axis=-1, keepdims=True))
    alpha = jnp.exp(m_old - m_new)
    p = jnp.exp(s - m_new)
    l_sc[...] = alpha * l_sc[...] + jnp.sum(p, axis=-1, keepdims=True)
    acc_sc[...] = alpha * acc_sc[...] + pv(p)
    m_sc[...] = m_new


def _own_head_lanes(out, rows_per_kv):
    row = lax.broadcasted_iota(jnp.int32, out.shape, 0)
    lane = lax.broadcasted_iota(jnp.int32, out.shape, 1)
    y = jnp.where(lane // HEAD_DIM == row // rows_per_kv, out, 0.0)
    while y.shape[1] > LANES:
        half = y.shape[1] // 2
        y = y[:, :half] + y[:, half:]
    return y + pltpu.roll(y, 64, 1)


def _mla_sample_kernel(pt_ref, *refs, cp, n_chunks, qk_dim, nope, scale):
    c_refs = refs[:cp]
    kr_refs = refs[cp:2 * cp]
    (qbd_ref, qr_ref, cnew_ref, krnew_ref, cosp_ref, sinp_ref, cosn_ref, sinn_ref, wukt_ref, wuv_ref,
     gkr_ref, mask_ref, o_ref, m_sc, l_sc, acc_sc) = refs[2 * cp:]
    j = pl.program_id(1)
    tn = cnew_ref.shape[1]
    n_heads = wuv_ref.shape[0]

    @pl.when(j == 0)
    def _():
        m_sc[...] = jnp.full(m_sc.shape, NEG, F32)
        l_sc[...] = jnp.zeros(l_sc.shape, F32)
        acc_sc[...] = jnp.zeros(acc_sc.shape, F32)

    def process(c, krt, cos_t, sin_t, mask):
        cb = c.astype(BF16)
        knt = _dot_nt(wukt_ref[...], cb)
        sq = knt * knt
        ssr = jnp.sum(krt * krt, axis=0, keepdims=True)
        inv = []
        for hh in range(n_heads):
            ss = jnp.sum(sq[nope * hh:nope * (hh + 1)], axis=0, keepdims=True) + ssr
            inv.append(lax.rsqrt(ss * (1.0 / qk_dim) + EPS) * scale)
        inv = _repeat_rows(jnp.concatenate(inv, axis=0), tn)
        krg = krt * gkr_ref[...]
        half = krt.shape[0] // 2
        krr = krg * cos_t + jnp.concatenate([krg[half:], krg[:half]], axis=0) * sin_t
        s = (_dot(qbd_ref[0], knt) + _dot(qr_ref[0], krr)) * inv
        if mask is not None:
            s = jnp.where(mask, s, NEG)
        _online_softmax(s, m_sc, l_sc, acc_sc, lambda p: _dot(p, cb))

    c = jnp.concatenate([r[...] for r in c_refs], axis=0)
    krt = jnp.concatenate([r[...] for r in kr_refs], axis=1)
    process(c, krt, cosp_ref[...], sinp_ref[...], None)

    @pl.when(j == n_chunks - 1)
    def _():
        process(cnew_ref[0], krnew_ref[0], cosn_ref[...], sinn_ref[...], mask_ref[...] > 0.5)
        pc = acc_sc[...] * (1.0 / l_sc[...])
        for hh in range(n_heads):
            o_ref[0, tn * hh:tn * (hh + 1), :] = _dot(pc[tn * hh:tn * (hh + 1), :], wuv_ref[hh])


def _mla_sample(page_table, cache_c, cache_krt, layer, qbd, qr, c_new, krt_new, cos_p, sin_p, cos_n, sin_n,
                wukt, wuv, gkr, mask, *, qk_dim, nope, scale):
    db, n_pages = page_table.shape
    cp = min(CHUNK_PAGES, n_pages)
    n_chunks = n_pages // cp
    tn = c_new.shape[1]
    n_heads = wuv.shape[0]
    rows = n_heads * tn
    fwd = lambda j: j
    per_seq = lambda a: pl.BlockSpec((1,) + a.shape[1:], lambda b, j, pt: (b,) + (0,) * (a.ndim - 1))
    full = lambda a: pl.BlockSpec(a.shape, lambda b, j, pt: (0,) * a.ndim)
    chunk_cols = lambda a: pl.BlockSpec((a.shape[0], cp * 128), lambda b, j, pt: (0, j))
    in_specs = (_page_specs(layer, cp, cache_c.shape[2:], fwd) + _page_specs(layer, cp, cache_krt.shape[2:], fwd)
                + [per_seq(qbd), per_seq(qr), per_seq(c_new), per_seq(krt_new), chunk_cols(cos_p), chunk_cols(sin_p),
                   full(cos_n), full(sin_n), full(wukt), full(wuv), full(gkr), full(mask)])
    grid_spec = pltpu.PrefetchScalarGridSpec(
        num_scalar_prefetch=1, grid=(db, n_chunks), in_specs=in_specs,
        out_specs=pl.BlockSpec((1, rows, wuv.shape[2]), lambda b, j, pt: (b, 0, 0)),
        scratch_shapes=[pltpu.VMEM((rows, 1), F32), pltpu.VMEM((rows, 1), F32), pltpu.VMEM((rows, cache_c.shape[-1]), F32)])
    return pl.pallas_call(
        functools.partial(_mla_sample_kernel, cp=cp, n_chunks=n_chunks, qk_dim=qk_dim, nope=nope, scale=scale),
        grid_spec=grid_spec,
        out_shape=jax.ShapeDtypeStruct((db, rows, wuv.shape[2]), F32),
        compiler_params=_cp(("parallel", "arbitrary")), name="mla_sample",
    )(page_table, *([cache_c] * cp), *([cache_krt] * cp), qbd, qr, c_new, krt_new, cos_p, sin_p, cos_n, sin_n,
      wukt, wuv, gkr, mask)


def _moba_sample_kernel(pt_ref, *refs, cp, n_chunks, n_blocks, rows_per_kv):
    k_refs = refs[:cp]
    v_refs = refs[cp:2 * cp]
    (q_ref, knew_ref, vnew_ref, t31_ref, tnew_ref, far_ref, ones_ref, o_ref,
     m_all, l_all, g_all, acc_all) = refs[2 * cp:]
    j = pl.program_id(1)
    q = q_ref[0]
    lane = lax.broadcasted_iota(jnp.int32, m_all.shape, 1)
    ppb = MOBA_BLOCK // 128
    bpc = cp // ppb

    @pl.when(j == 0)
    def _():
        m_all[...] = jnp.full(m_all.shape, NEG, F32)
        l_all[...] = jnp.zeros(l_all.shape, F32)
        g_all[...] = jnp.zeros(g_all.shape, F32)

    def block(n, s, pv):
        m = jnp.max(s, axis=-1, keepdims=True)
        p = jnp.exp(s - m)
        hit = lane == n
        m_all[...] = jnp.where(hit, m, m_all[...])
        l_all[...] = jnp.where(hit, jnp.sum(p, axis=-1, keepdims=True), l_all[...])
        acc_all[n] = pv(p)

    for i in range(bpc):
        n = j * bpc + i
        kt = jnp.concatenate([k_refs[ppb * i + u][...] for u in range(ppb)], axis=1)
        vt = jnp.concatenate([v_refs[ppb * i + u][...] for u in range(ppb)], axis=1)
        kmean = _dot_exact_rhs(kt, ones_ref[...], 3) * (1.0 / MOBA_BLOCK)
        gate = _dot_exact_lhs(q, kmean, 2)
        g_all[...] = jnp.where(lane == n, gate, g_all[...])
        bias = jnp.where(n == n_blocks - 1, t31_ref[...], far_ref[...])
        block(n, _dot(q, kt) + bias, lambda p: _dot_nt(p, vt))

    @pl.when(j == n_chunks - 1)
    def _():
        block(n_blocks, _dot_nt(q, knew_ref[0]) + tnew_ref[...], lambda p: _dot(p, vnew_ref[0]))
        picked = _top_blocks(g_all[...], lane, n_blocks, min(MOBA_TOPK, n_blocks + 1))
        selected = jnp.logical_or(picked, lane == n_blocks)
        m_sel = jnp.where(selected, m_all[...], NEG)
        m_tot = jnp.max(m_sel, axis=-1, keepdims=True)
        w = jnp.where(selected, jnp.exp(m_sel - m_tot), 0.0)
        w = w * (1.0 / jnp.sum(w * l_all[...], axis=-1, keepdims=True))
        out = jnp.zeros(acc_all.shape[1:], F32)
        for n in range(n_blocks + 1):
            out = out + w[:, n:n + 1] * acc_all[n]
        o_ref[0] = _own_head_lanes(out, rows_per_kv)


def _moba_sample(page_table, cache_kt, cache_vt, layer, q, k_new, v_new, t31, tnew, far, *, rows_per_kv):
    db, n_pages = page_table.shape
    cp = min(CHUNK_PAGES, n_pages)
    n_chunks = n_pages // cp
    n_blocks = n_pages * 128 // MOBA_BLOCK
    assert n_blocks + 1 <= LANES
    rows = q.shape[1]
    ones = jnp.ones((MOBA_BLOCK, LANES), BF16)
    fwd = lambda j: j
    per_seq = lambda a: pl.BlockSpec((1,) + a.shape[1:], lambda b, j, pt: (b,) + (0,) * (a.ndim - 1))
    full = lambda a: pl.BlockSpec(a.shape, lambda b, j, pt: (0,) * a.ndim)
    in_specs = (_page_specs(layer, cp, cache_kt.shape[2:], fwd) + _page_specs(layer, cp, cache_vt.shape[2:], fwd)
                + [per_seq(q), per_seq(k_new), per_seq(v_new), full(t31), full(tnew), full(far), full(ones)])
    grid_spec = pltpu.PrefetchScalarGridSpec(
        num_scalar_prefetch=1, grid=(db, n_chunks), in_specs=in_specs,
        out_specs=pl.BlockSpec((1, rows, LANES), lambda b, j, pt: (b, 0, 0)),
        scratch_shapes=[pltpu.VMEM((rows, LANES), F32), pltpu.VMEM((rows, LANES), F32),
                        pltpu.VMEM((rows, LANES), F32), pltpu.VMEM((n_blocks + 1, rows, LANES), F32)])
    return pl.pallas_call(
        functools.partial(_moba_sample_kernel, cp=cp, n_chunks=n_chunks, n_blocks=n_blocks, rows_per_kv=rows_per_kv),
        grid_spec=grid_spec, out_shape=jax.ShapeDtypeStruct((db, rows, LANES), F32),
        compiler_params=_cp(("parallel", "arbitrary")), name="moba_sample",
    )(page_table, *([cache_kt] * cp), *([cache_vt] * cp), q, k_new, v_new, t31, tnew, far, ones)


def _fox_sample_kernel(pt_ref, *refs, cp, n_chunks, rows_per_kv):
    k_refs = refs[:cp]
    v_refs = refs[cp:2 * cp]
    lf_refs = refs[2 * cp:3 * cp]
    (q_ref, knew_ref, vnew_ref, lftnew_ref, rep_ref, tri_ref, tmask_ref, cmask_ref, o_ref,
     m_sc, l_sc, acc_sc, carry_sc, ncol_sc) = refs[3 * cp:]
    j = pl.program_id(1)
    q = q_ref[0]
    n_h = lftnew_ref.shape[1]
    reps = q.shape[0] // n_h

    @pl.when(j == 0)
    def _():
        m_sc[...] = jnp.full(m_sc.shape, NEG, F32)
        l_sc[...] = jnp.zeros(l_sc.shape, F32)
        acc_sc[...] = jnp.zeros(acc_sc.shape, F32)
        carry_sc[...] = jnp.zeros(carry_sc.shape, F32)
        n_t = _dot_exact_rhs(lftnew_ref[0], tri_ref[...], 3)
        nmat = _dot_exact_lhs(rep_ref[...], n_t, 3)
        ncol = jnp.sum(nmat * tmask_ref[...], axis=-1, keepdims=True)
        ncol_sc[...] = ncol
        s = _dot_nt(q, knew_ref[0]) + (ncol - nmat)
        _online_softmax(jnp.where(cmask_ref[...] > 0.5, s, NEG), m_sc, l_sc, acc_sc, lambda p: _dot(p, vnew_ref[0]))

    kt = jnp.concatenate([r[...] for r in k_refs], axis=1)
    vt = jnp.concatenate([r[...] for r in v_refs], axis=1)
    lft = jnp.concatenate([r[...] for r in lf_refs], axis=1)
    tk = lft.shape[1]
    lane = lax.broadcasted_iota(jnp.int32, lft.shape, 1)
    x = lft
    sh = 1
    while sh < tk:
        if sh % LANES == 0:
            shifted = jnp.concatenate([x[:, sh:], jnp.zeros((n_h, sh), F32)], axis=1)
        else:
            shifted = jnp.where(lane < tk - sh, pltpu.roll(x, tk - sh, 1), 0.0)
        x = x + shifted
        sh *= 2
    carry = carry_sc[...]
    later = x - lft + carry
    carry_sc[...] = carry + x[:, 0:1]
    s = _dot(q, kt) + _repeat_rows(later, reps) + ncol_sc[...]
    _online_softmax(s, m_sc, l_sc, acc_sc, lambda p: _dot_nt(p, vt))

    @pl.when(j == n_chunks - 1)
    def _():
        o_ref[0] = _own_head_lanes(acc_sc[...] * (1.0 / l_sc[...]), rows_per_kv)


def _fox_sample(page_table, cache_kt, cache_vt, cache_lft, layer, q, k_new, v_new, lft_new, rep, tri, tmask, cmask,
                *, rows_per_kv):
    db, n_pages = page_table.shape
    cp = min(CHUNK_PAGES, n_pages)
    n_chunks = n_pages // cp
    rows = q.shape[1]
    rev = lambda j: n_chunks - 1 - j
    per_seq = lambda a: pl.BlockSpec((1,) + a.shape[1:], lambda b, j, pt: (b,) + (0,) * (a.ndim - 1))
    full = lambda a: pl.BlockSpec(a.shape, lambda b, j, pt: (0,) * a.ndim)
    in_specs = (_page_specs(layer, cp, cache_kt.shape[2:], rev) + _page_specs(layer, cp, cache_vt.shape[2:], rev)
                + _page_specs(layer, cp, cache_lft.shape[2:], rev)
                + [per_seq(q), per_seq(k_new), per_seq(v_new), per_seq(lft_new), full(rep), full(tri), full(tmask), full(cmask)])
    grid_spec = pltpu.PrefetchScalarGridSpec(
        num_scalar_prefetch=1, grid=(db, n_chunks), in_specs=in_specs,
        out_specs=pl.BlockSpec((1, rows, LANES), lambda b, j, pt: (b, 0, 0)),
        scratch_shapes=[pltpu.VMEM((rows, 1), F32), pltpu.VMEM((rows, 1), F32), pltpu.VMEM((rows, q.shape[2]), F32),
                        pltpu.VMEM((lft_new.shape[1], 1), F32), pltpu.VMEM((rows, 1), F32)])
    return pl.pallas_call(
        functools.partial(_fox_sample_kernel, cp=cp, n_chunks=n_chunks, rows_per_kv=rows_per_kv),
        grid_spec=grid_spec, out_shape=jax.ShapeDtypeStruct((db, rows, LANES), F32),
        compiler_params=_cp(("parallel", "arbitrary")), name="fox_sample",
    )(page_table, *([cache_kt] * cp), *([cache_vt] * cp), *([cache_lft] * cp), q, k_new, v_new, lft_new,
      rep, tri, tmask, cmask)


def _mix_mlp_kernel(x_ref, o_ref, wo_ref, g_ref, up_ref, down_ref, y_ref, x1_sc, h_sc, acc_sc):
    f = pl.program_id(1)

    @pl.when(f == 0)
    def _():
        x1 = x_ref[...] + jnp.dot(o_ref[...], wo_ref[...], preferred_element_type=F32)
        x1_sc[...] = x1
        h_sc[...] = _rms(x1, g_ref[...]).astype(BF16)
        acc_sc[...] = jnp.zeros(acc_sc.shape, F32)

    u = jnp.maximum(jnp.dot(h_sc[...], up_ref[...], preferred_element_type=F32), 0.0)
    acc_sc[...] += jnp.dot((u * u).astype(BF16), down_ref[...], preferred_element_type=F32)

    @pl.when(f == pl.num_programs(1) - 1)
    def _():
        y_ref[...] = x1_sc[...] + acc_sc[...]


def _mix_mlp(x, o, wo, g, up, down, *, tm):
    n, d = x.shape
    dff = up.shape[1]
    tf = min(FF_TILE, dff)
    return pl.pallas_call(
        _mix_mlp_kernel, grid=(n // tm, dff // tf),
        in_specs=[pl.BlockSpec((tm, d), lambda i, f: (i, 0)), pl.BlockSpec((tm, o.shape[1]), lambda i, f: (i, 0)),
                  pl.BlockSpec(wo.shape, lambda i, f: (0, 0)), pl.BlockSpec(g.shape, lambda i, f: (0, 0)),
                  pl.BlockSpec((d, tf), lambda i, f: (0, f)), pl.BlockSpec((tf, d), lambda i, f: (f, 0))],
        out_specs=pl.BlockSpec((tm, d), lambda i, f: (i, 0)),
        out_shape=jax.ShapeDtypeStruct((n, d), F32),
        scratch_shapes=[pltpu.VMEM((tm, d), F32), pltpu.VMEM((tm, d), BF16), pltpu.VMEM((tm, d), F32)],
        compiler_params=_cp(("parallel", "arbitrary")), name="mix_mlp",
    )(x, o, wo, g, up, down)


def _t5_bucket(rel):
    n = jnp.maximum(rel, 0)
    exact = REL_BUCKETS // 2
    scaled = jnp.log(jnp.maximum(n, 1).astype(F32) / exact) / math.log(REL_MAX_DIST / exact)
    large = exact + (scaled * (REL_BUCKETS - exact)).astype(jnp.int32)
    return jnp.where(n < exact, n, jnp.minimum(large, REL_BUCKETS - 1))


def _rope_tables(pos, half):
    inv = ROPE_THETA ** (-jnp.arange(half, dtype=F32) / half)
    ang = pos.astype(F32)[:, None] * inv[None, :]
    return jnp.cos(ang), jnp.sin(ang)


def _pick_tile(n, cap):
    t = cap
    while n % t:
        t //= 2
    return t


def _pages_on_lanes(cache):
    nd = cache.ndim
    t = jnp.transpose(cache, (0, 1) + tuple(range(3, nd)) + (2,))
    return t.reshape(t.shape[:2] + (-1, t.shape[-1]))


def kernel(x_prompt, x_sample, cache_mla_latent, cache_mla_krope, cache_moba_k, cache_moba_v, cache_fox_k, cache_fox_v, cache_fox_logf, page_table, rel_bias, g_attn, g_mlp, w_mlp_up, w_mlp_down, w_in_even, g_q_lat, g_kv_lat, w_uq, w_uk, w_uv, g_mla_q, g_mla_k, g_moba_q, g_moba_k, w_o_even, w_in_odd, b_forget, g_fox_q, g_fox_k, w_o_odd):
    B, S, D = x_prompt.shape
    DB, TN, _ = x_sample.shape
    depth = g_attn.shape[0]
    n_pages = page_table.shape[1]
    page = cache_mla_latent.shape[2]
    past = n_pages * page
    n_p, n_s = B * S, DB * TN
    n = n_p + n_s
    q_lora, kv_lora = g_q_lat.shape[1], g_kv_lat.shape[1]
    mla_heads, nope, mla_v = w_uk.shape[2], w_uk.shape[3], w_uv.shape[3]
    qk_dim = w_uq.shape[3]
    rope_d = qk_dim - nope
    half = rope_d // 2
    moba_kv = cache_moba_k.shape[3]
    moba_heads = rel_bias.shape[1]
    moba_g = moba_heads // moba_kv
    fox_heads = b_forget.shape[1]
    fox_kv = cache_fox_k.shape[3]
    fox_g = fox_heads // fox_kv
    mla_scale = float(qk_dim) ** -0.5
    att_scale = HEAD_DIM ** -0.5
    assert (q_lora, kv_lora, mla_heads, nope, rope_d, mla_v) == (256, 256, 8, 64, 32, 64)
    assert (moba_heads, moba_kv, fox_heads, fox_kv, page, TN) == (8, 2, 16, 4, 128, 8)
    assert past % MOBA_BLOCK == 0 and S % 512 == 0 and n_p % 256 == 0 and n_s % 256 == 0
    assert MOBA_BLOCK + 1 >= REL_MAX_DIST
    tm = _pick_tile(math.gcd(n_p, n_s), TOKEN_TILE)

    pos = jnp.concatenate([jnp.tile(jnp.arange(S, dtype=jnp.int32), B),
                           jnp.tile(past + jnp.arange(TN, dtype=jnp.int32), DB)])
    cos, sin = _rope_tables(pos, half)
    ones64 = jnp.ones((n, 64), F32)
    zeros32 = jnp.zeros((n, 32), F32)
    cos_t = jnp.concatenate([ones64, cos, cos, zeros32], axis=1)
    sin_t = jnp.concatenate([jnp.zeros((n, 64), F32), -sin, sin, zeros32], axis=1)
    kcos, ksin = _rope_tables(jnp.arange(past + TN, dtype=jnp.int32), half)
    kcos_t = jnp.concatenate([kcos, kcos], axis=1).T
    ksin_t = jnp.concatenate([-ksin, ksin], axis=1).T

    bd = (jnp.arange(512)[:, None] // HEAD_DIM == jnp.arange(512)[None, :] // HEAD_DIM).astype(BF16)
    t_new = jnp.arange(TN)
    t_mla = jnp.arange(mla_heads * TN) % TN
    mla_mask = (t_new[None, :] <= t_mla[:, None]).astype(F32)
    r_fox = jnp.arange(fox_heads * TN)
    rep_fox = (r_fox[:, None] // TN == jnp.arange(fox_heads)[None, :]).astype(BF16)
    tri_new = (t_new[:, None] <= t_new[None, :]).astype(BF16)
    tmask_fox = (t_new[None, :] == (r_fox % TN)[:, None]).astype(F32)
    cmask_fox = (t_new[None, :] <= (r_fox % TN)[:, None]).astype(F32)

    ii = jnp.arange(MOBA_BLOCK)
    rel_own = ii[:, None] - ii[None, :]
    b_own = jnp.where(rel_own >= 0, _t5_bucket(rel_own), -1)
    b_prev = _t5_bucket(rel_own + MOBA_BLOCK)
    b_31 = _t5_bucket(MOBA_BLOCK + t_new[:, None] - ii[None, :])
    rel_new = t_new[:, None] - t_new[None, :]
    b_new = jnp.where(rel_new >= 0, _t5_bucket(rel_new), -1)
    b_new = jnp.pad(b_new, ((0, 0), (0, LANES - TN)), constant_values=-1)
    town, tprev, t31, tnew, far = _bias_tables(rel_bias.astype(F32), b_own.astype(jnp.int32), b_prev.astype(jnp.int32),
                                               b_31.astype(jnp.int32), b_new.astype(jnp.int32))
    t31_s = t31.reshape(moba_heads * TN, MOBA_BLOCK)
    tnew_s = tnew.reshape(moba_heads * TN, LANES)[:, :TN]
    far_s = far.reshape(moba_heads * 8, LANES)[:, :1]

    krope_t = _pages_on_lanes(cache_mla_krope)
    moba_kt, moba_vt = _pages_on_lanes(cache_moba_k), _pages_on_lanes(cache_moba_v)
    fox_kt, fox_vt = _pages_on_lanes(cache_fox_k), _pages_on_lanes(cache_fox_v)
    fox_lft = _pages_on_lanes(cache_fox_logf)

    pt = page_table.astype(jnp.int32)
    x = jnp.concatenate([x_prompt.reshape(n_p, D), x_sample.reshape(n_s, D)], axis=0)
    outs = {k: [] for k in ("lat", "kr", "mk", "mv", "fk", "fv", "lf")}

    for layer in range(depth):
        ga = g_attn[layer][None, :]
        if layer % 2 == 0:
            e = layer // 2
            w = w_in_even[e]
            wp = jnp.concatenate([w[:, 0:512], w[:, 544:1312], jnp.zeros((D, nope), F32), w[:, 512:544],
                                  jnp.zeros((D, LANES - qk_dim), F32)], axis=1).astype(BF16)
            wuq = jnp.pad(w_uq[e], ((0, 0), (0, 0), (0, LANES - qk_dim))).reshape(q_lora, mla_heads * LANES).astype(BF16)
            wuk_pad = jnp.pad(w_uk[e], ((0, 0), (0, 0), (0, LANES - nope))).reshape(kv_lora, mla_heads * LANES).astype(BF16)
            wukt = w_uk[e].transpose(1, 2, 0).reshape(mla_heads * nope, kv_lora).astype(BF16)
            wuv = w_uv[e].reshape(kv_lora, mla_heads * mla_v).astype(BF16)
            wuv_h = w_uv[e].transpose(1, 0, 2).astype(BF16)
            gq = jnp.pad(g_mla_q[e], (0, LANES - qk_dim))[None, :]
            gkn = jnp.concatenate([g_mla_k[e][:nope], jnp.ones((LANES - nope,), F32)])[None, :]
            gkr = jnp.concatenate([jnp.zeros((nope,), F32), g_mla_k[e][nope:], jnp.zeros((LANES - qk_dim,), F32)])[None, :]
            gmq = (jnp.tile(g_moba_q[e], moba_heads) * att_scale)[None, :]
            gmk = jnp.tile(g_moba_k[e], moba_kv)[None, :]
            (c, krb, mk, mv, qmla, kmla, vmla, qmo, kmo, vmo, kmean) = _even_proj(
                x, ga, wp, g_q_lat[e][None, :], g_kv_lat[e][None, :], wuq, gq, gkn, gkr, wuk_pad, wuv, gmq, gmk, bd,
                cos_t, sin_t, tm=tm, qk_dim=qk_dim, scale=mla_scale)
            kr = krb[:, 64:64 + rope_d]
            o1_p = _flash("mla", qmla, kmla, vmla, (), b=B, s=S, t=512)
            nblk = S // MOBA_BLOCK
            km_p = kmean.reshape(n // MOBA_BLOCK, 128)[:B * nblk].reshape(B, nblk, 128)
            km_p = jnp.pad(km_p, ((0, 0), (0, LANES - nblk), (0, 0)))
            o2_p = _flash("moba", qmo, kmo, vmo, (km_p, town, tprev, far), b=B, s=S, t=MOBA_BLOCK)
            qs = qmla[n_p:].reshape(DB, TN, mla_heads, LANES).transpose(0, 2, 1, 3)
            qbd = jnp.einsum('bhtd,hj->bhtjd', qs[..., :nope], jnp.eye(mla_heads, dtype=BF16))
            qbd = qbd.reshape(DB, mla_heads * TN, mla_heads * nope)
            qr = qs[..., nope:qk_dim].reshape(DB, mla_heads * TN, rope_d)
            o1_s = _mla_sample(pt, cache_mla_latent, krope_t, e, qbd, qr,
                               c[n_p:].reshape(DB, TN, kv_lora), kr[n_p:].reshape(DB, TN, rope_d).transpose(0, 2, 1),
                               kcos_t[:, :past], ksin_t[:, :past], kcos_t[:, past:], ksin_t[:, past:],
                               wukt, wuv_h, g_mla_k[e][nope:][:, None], mla_mask,
                               qk_dim=qk_dim, nope=nope, scale=mla_scale)
            o1_s = o1_s.reshape(DB, mla_heads, TN, mla_v).transpose(0, 2, 1, 3).reshape(n_s, mla_heads * mla_v)
            q2 = qmo[n_p:].reshape(DB, TN, moba_kv, moba_g, HEAD_DIM).transpose(0, 2, 3, 1, 4)
            q2 = jnp.einsum('bkgtd,kj->bkgtjd', q2, jnp.eye(moba_kv, dtype=BF16)).reshape(DB, moba_heads * TN, moba_kv * HEAD_DIM)
            o2_s = _moba_sample(pt, moba_kt, moba_vt, e, q2, mk[n_p:].reshape(DB, TN, -1), mv[n_p:].reshape(DB, TN, -1),
                                t31_s, tnew_s, far_s, rows_per_kv=moba_g * TN)
            o2_s = o2_s[:, :, :HEAD_DIM].reshape(DB, moba_heads, TN, HEAD_DIM).transpose(0, 2, 1, 3)
            o2_s = o2_s.reshape(n_s, moba_heads * HEAD_DIM)
            o = jnp.concatenate([jnp.concatenate([o1_p, o2_p], axis=1),
                                 jnp.concatenate([o1_s.astype(BF16), o2_s.astype(BF16)], axis=1)], axis=0)
            wo = w_o_even[e].astype(BF16)
            outs["lat"].append(c); outs["kr"].append(kr); outs["mk"].append(mk); outs["mv"].append(mv)
        else:
            jl = layer // 2
            w = w_in_odd[jl]
            wp = jnp.concatenate([w, jnp.zeros((D, LANES - fox_heads), F32)], axis=1).astype(BF16)
            bf = jnp.pad(b_forget[jl], (0, LANES - fox_heads))[None, :]
            gq = (jnp.tile(g_fox_q[jl], fox_heads) * att_scale)[None, :]
            gk = jnp.tile(g_fox_k[jl], fox_kv)[None, :]
            q, k, v, lf, kb, vb = _odd_proj(x, ga, wp, bf, gq, gk, bd, tm=tm)
            cum, cumt = _prompt_cumsum(lf[:n_p].reshape(B, S, LANES))
            o_p = _flash("fox", q, kb, vb, (cum, cumt), b=B, s=S, t=512)
            qf = q[n_p:].reshape(DB, TN, fox_kv, fox_g, HEAD_DIM).transpose(0, 2, 3, 1, 4)
            qf = jnp.einsum('bkgtd,kj->bkgtjd', qf, jnp.eye(fox_kv, dtype=BF16)).reshape(DB, fox_heads * TN, fox_kv * HEAD_DIM)
            lfh = lf[:, :fox_heads]
            o_s = _fox_sample(pt, fox_kt, fox_vt, fox_lft, jl, qf,
                              k[n_p:].reshape(DB, TN, -1), v[n_p:].reshape(DB, TN, -1),
                              lfh[n_p:].reshape(DB, TN, fox_heads).transpose(0, 2, 1),
                              rep_fox, tri_new, tmask_fox, cmask_fox, rows_per_kv=fox_g * TN)
            o_s = o_s[:, :, :HEAD_DIM].reshape(DB, fox_heads, TN, HEAD_DIM).transpose(0, 2, 1, 3).reshape(n_s, fox_heads * HEAD_DIM)
            o = jnp.concatenate([o_p, o_s.astype(BF16)], axis=0)
            wo = w_o_odd[jl].astype(BF16)
            outs["fk"].append(k); outs["fv"].append(v); outs["lf"].append(lfh)
        x = _mix_mlp(x, o, wo, g_mlp[layer][None, :], w_mlp_up[layer].astype(BF16), w_mlp_down[layer].astype(BF16), tm=tm)

    def split(name, tail):
        a = jnp.stack(outs[name])
        return a[:, :n_p].reshape((a.shape[0], B, S) + tail), a[:, n_p:].reshape((a.shape[0], DB, TN) + tail)

    lat_p, lat_s = split("lat", (kv_lora,))
    kr_p, kr_s = split("kr", (rope_d,))
    mk_p, mk_s = split("mk", (moba_kv, HEAD_DIM))
    mv_p, mv_s = split("mv", (moba_kv, HEAD_DIM))
    fk_p, fk_s = split("fk", (fox_kv, HEAD_DIM))
    fv_p, fv_s = split("fv", (fox_kv, HEAD_DIM))
    lf_p, lf_s = split("lf", (fox_heads,))
    return (x[:n_p].reshape(B, S, D), x[n_p:].reshape(DB, TN, D), lat_p, lat_s, kr_p, kr_s, mk_p, mk_s, mv_p, mv_s,
            fk_p, fk_s, fv_p, fv_s, lf_p, lf_s)
```

```python
import functools
import math

import jax
import jax.numpy as jnp
from jax import lax
from jax.experimental import pallas as pl
from jax.experimental.pallas import tpu as pltpu

F32 = jnp.float32
BF16 = jnp.bfloat16
NEG = -1e30
EPS = 1e-6
LOG2E = math.log2(math.e)
ROPE_THETA = 10000.0
HEAD_DIM = 64
MOBA_BLOCK = 256
MOBA_TOPK = 3
REL_BUCKETS = 32
REL_MAX_DIST = 128
LANES = 128
TOKEN_TILE = 512
FF_TILE = 1024
CHUNK_PAGES = 16
VMEM_LIMIT = 56 * 1024 * 1024


def _cp(sem, vmem=VMEM_LIMIT):
    return pltpu.CompilerParams(dimension_semantics=sem, vmem_limit_bytes=vmem)


def _dot(a, b):
    return jnp.dot(a.astype(BF16), b.astype(BF16), preferred_element_type=F32)


def _dot_nt(a, b):
    return lax.dot_general(a.astype(BF16), b.astype(BF16), (((1,), (1,)), ((), ())),
                           preferred_element_type=F32)


def _dot_tn(a, b):
    return lax.dot_general(a.astype(BF16), b.astype(BF16), (((0,), (0,)), ((), ())),
                           preferred_element_type=F32)


def _split(a, n):
    parts = []
    r = a.astype(F32)
    for _ in range(n):
        p = r.astype(BF16)
        parts.append(p)
        r = r - p.astype(F32)
    return parts


def _dot_exact_rhs(a, b, n=3):
    return sum(jnp.dot(p, b, preferred_element_type=F32) for p in _split(a, n))


def _dot_exact_lhs(a, b, n=3):
    return sum(jnp.dot(a, p, preferred_element_type=F32) for p in _split(b, n))


def _dot_nt_exact_lhs(a, b, n=3):
    return sum(lax.dot_general(a, p, (((1,), (1,)), ((), ())), preferred_element_type=F32)
               for p in _split(b, n))


def _rms(v, g):
    return v * lax.rsqrt(jnp.mean(v * v, axis=-1, keepdims=True) + EPS) * g


def _head_rms(v, g, bd, width):
    step = min(width, bd.shape[0])
    outs = []
    for s in range(0, width, step):
        blk = v[:, s:s + step]
        ss = _dot_exact_rhs(blk * blk, bd[:step, :step], 2)
        outs.append(blk * lax.rsqrt(ss * (1.0 / HEAD_DIM) + EPS))
    out = outs[0] if len(outs) == 1 else jnp.concatenate(outs, axis=-1)
    return out * g


def _rot_block(v, cos_t, sin_t):
    lane = lax.broadcasted_iota(jnp.int32, v.shape, 1)
    partner = jnp.where(lane < 80, pltpu.roll(v, LANES - 16, 1), pltpu.roll(v, 16, 1))
    return v * cos_t + partner * sin_t


def _repeat_rows(x, reps):
    return jnp.concatenate([jnp.broadcast_to(x[i:i + 1, :], (reps, x.shape[1])) for i in range(x.shape[0])], axis=0)


def _top_blocks(gate, blk, n_past, n_pick, axis):
    g = jnp.where(blk < n_past, gate, NEG)
    selected = jnp.zeros(gate.shape, jnp.bool_)
    for r in range(n_pick):
        mx = jnp.max(g, axis=axis, keepdims=True)
        idx = jnp.min(jnp.where(g == mx, blk, gate.shape[axis]), axis=axis, keepdims=True)
        hit = blk == idx
        selected = jnp.logical_or(selected, jnp.logical_and(hit, r < n_past))
        g = jnp.where(hit, -3e38, g)
    return selected


def _even_proj_kernel(x_ref, ga_ref, wp_ref, gql_ref, gkv_ref, wuq_ref, gq_ref, gkn_ref, gkr_ref,
                      wuk_ref, wuv_ref, gmq_ref, gmk_ref, bd_ref, cos_ref, sin_ref,
                      c_ref, krb_ref, mk_ref, mv_ref, qmla_ref, kmla_ref, vmla_ref,
                      qmo_ref, kmo_ref, vmo_ref, kmean_ref, *, n_heads, qk_dim, scale):
    x = x_ref[...]
    h = _rms(x, ga_ref[...]).astype(BF16)
    y = jnp.dot(h, wp_ref[...], preferred_element_type=F32)
    cos_t = cos_ref[...]
    sin_t = sin_ref[...]
    bd = bd_ref[...]

    c = _rms(y[:, 256:512], gkv_ref[...])
    c_ref[...] = c
    cb = c.astype(BF16)

    ql = _rms(y[:, 0:256], gql_ref[...]).astype(BF16)
    q = jnp.dot(ql, wuq_ref[...], preferred_element_type=F32)
    gq = gq_ref[...]
    gkn = gkn_ref[...]
    inv_d = 1.0 / qk_dim
    for hh in range(n_heads):
        sl = slice(LANES * hh, LANES * (hh + 1))
        qh = q[:, sl]
        qh = qh * lax.rsqrt(jnp.sum(qh * qh, axis=-1, keepdims=True) * inv_d + EPS) * gq
        qmla_ref[:, sl] = _rot_block(qh * gkn, cos_t, sin_t).astype(BF16)

    krb = y[:, 1280:1408]
    krb_ref[...] = krb
    ssr = jnp.sum(krb * krb, axis=-1, keepdims=True)
    krr = _rot_block(krb * gkr_ref[...], cos_t, sin_t)
    kn = jnp.dot(cb, wuk_ref[...], preferred_element_type=F32)
    for hh in range(n_heads):
        sl = slice(LANES * hh, LANES * (hh + 1))
        kh = kn[:, sl]
        inv = lax.rsqrt((jnp.sum(kh * kh, axis=-1, keepdims=True) + ssr) * inv_d + EPS) * scale
        kmla_ref[:, sl] = ((kh + krr) * inv).astype(BF16)
    vmla_ref[...] = jnp.dot(cb, wuv_ref[...], preferred_element_type=F32).astype(BF16)

    mq = _head_rms(y[:, 512:1024], gmq_ref[...], bd, 512)
    qmo_ref[...] = mq.astype(BF16)
    mk = _head_rms(y[:, 1024:1152], gmk_ref[...], bd, 128)
    mk_ref[...] = mk
    kmo_ref[...] = mk.astype(BF16)
    mv = y[:, 1152:1280]
    mv_ref[...] = mv
    vmo_ref[...] = mv.astype(BF16)
    tm = x.shape[0]
    for i in range(tm // MOBA_BLOCK):
        kmean_ref[0, i:i + 1, :] = jnp.mean(mk[MOBA_BLOCK * i:MOBA_BLOCK * (i + 1)], axis=0, keepdims=True)


def _even_proj(x, ga, wp, gql, gkv, wuq, gq, gkn, gkr, wuk, wuv, gmq, gmk, bd, cos_t, sin_t, *, tm, qk_dim, scale):
    n, d = x.shape
    nt = n // tm
    row = lambda c: pl.BlockSpec((tm, c), lambda i: (i, 0))
    full = lambda a: pl.BlockSpec(a.shape, lambda i: (0,) * a.ndim)
    consts = (ga, wp, gql, gkv, wuq, gq, gkn, gkr, wuk, wuv, gmq, gmk, bd)
    out_shape = (
        jax.ShapeDtypeStruct((n, 256), F32), jax.ShapeDtypeStruct((n, 128), F32),
        jax.ShapeDtypeStruct((n, 128), F32), jax.ShapeDtypeStruct((n, 128), F32),
        jax.ShapeDtypeStruct((n, 1024), BF16), jax.ShapeDtypeStruct((n, 1024), BF16),
        jax.ShapeDtypeStruct((n, 512), BF16), jax.ShapeDtypeStruct((n, 512), BF16),
        jax.ShapeDtypeStruct((n, 128), BF16), jax.ShapeDtypeStruct((n, 128), BF16),
        jax.ShapeDtypeStruct((nt, tm // MOBA_BLOCK, 128), F32),
    )
    out_specs = (row(256), row(128), row(128), row(128), row(1024), row(1024), row(512), row(512),
                 row(128), row(128), pl.BlockSpec((1, tm // MOBA_BLOCK, 128), lambda i: (i, 0, 0)))
    return pl.pallas_call(
        functools.partial(_even_proj_kernel, n_heads=8, qk_dim=qk_dim, scale=scale),
        grid=(nt,),
        in_specs=[row(d)] + [full(a) for a in consts] + [row(128), row(128)],
        out_specs=out_specs, out_shape=out_shape,
        compiler_params=_cp(("parallel",)), name="even_proj",
    )(x, *consts, cos_t, sin_t)


def _odd_proj_kernel(x_ref, ga_ref, wp_ref, bf_ref, gq_ref, gk_ref, bd_ref,
                     q_ref, k_ref, v_ref, lf_ref, kb_ref, vb_ref):
    x = x_ref[...]
    h = _rms(x, ga_ref[...]).astype(BF16)
    y = jnp.dot(h, wp_ref[...], preferred_element_type=F32)
    bd = bd_ref[...]
    q_ref[...] = _head_rms(y[:, 0:1024], gq_ref[...], bd, 1024).astype(BF16)
    k = _head_rms(y[:, 1024:1280], gk_ref[...], bd, 256)
    k_ref[...] = k
    kb_ref[...] = k.astype(BF16)
    v = y[:, 1280:1536]
    v_ref[...] = v
    vb_ref[...] = v.astype(BF16)
    z = y[:, 1536:1664] + bf_ref[...]
    lf_ref[...] = jnp.minimum(z, 0.0) - jnp.log1p(jnp.exp(-jnp.abs(z)))


def _odd_proj(x, ga, wp, bf, gq, gk, bd, *, tm):
    n, d = x.shape
    nt = n // tm
    row = lambda c: pl.BlockSpec((tm, c), lambda i: (i, 0))
    full = lambda a: pl.BlockSpec(a.shape, lambda i: (0,) * a.ndim)
    consts = (ga, wp, bf, gq, gk, bd)
    out_shape = (jax.ShapeDtypeStruct((n, 1024), BF16), jax.ShapeDtypeStruct((n, 256), F32),
                 jax.ShapeDtypeStruct((n, 256), F32), jax.ShapeDtypeStruct((n, 128), F32),
                 jax.ShapeDtypeStruct((n, 256), BF16), jax.ShapeDtypeStruct((n, 256), BF16))
    return pl.pallas_call(
        _odd_proj_kernel, grid=(nt,),
        in_specs=[row(d)] + [full(a) for a in consts],
        out_specs=(row(1024), row(256), row(256), row(128), row(256), row(256)), out_shape=out_shape,
        compiler_params=_cp(("parallel",)), name="odd_proj",
    )(x, *consts)


def _cumsum_kernel(lf_ref, tri_ref, eye_ref, cum_ref, cumt_ref, carry_ref, *, blk):
    carry_ref[...] = jnp.zeros_like(carry_ref)
    s = lf_ref.shape[1]
    tri = tri_ref[...]
    eye = eye_ref[...]
    for i in range(s // blk):
        lf = lf_ref[0, blk * i:blk * (i + 1), :]
        cum = _dot_exact_lhs(tri, lf, 3) + carry_ref[...]
        carry_ref[...] = cum[blk - 1:blk, :]
        cum_ref[0, blk * i:blk * (i + 1), :] = cum
        cumt_ref[0, :, blk * i:blk * (i + 1)] = _dot_nt_exact_lhs(eye, cum, 3)


def _prompt_cumsum(lf):
    b, s, w = lf.shape
    blk = 256
    tri = jnp.tril(jnp.ones((blk, blk), F32)).astype(BF16)
    eye = jnp.eye(w, dtype=BF16)
    return pl.pallas_call(
        functools.partial(_cumsum_kernel, blk=blk), grid=(b,),
        in_specs=[pl.BlockSpec((1, s, w), lambda i: (i, 0, 0)),
                  pl.BlockSpec((blk, blk), lambda i: (0, 0)), pl.BlockSpec((w, w), lambda i: (0, 0))],
        out_specs=(pl.BlockSpec((1, s, w), lambda i: (i, 0, 0)), pl.BlockSpec((1, w, s), lambda i: (i, 0, 0))),
        out_shape=(jax.ShapeDtypeStruct((b, s, w), F32), jax.ShapeDtypeStruct((b, w, s), F32)),
        scratch_shapes=[pltpu.VMEM((1, w), F32)],
        compiler_params=_cp(("parallel",)), name="prompt_cumsum",
    )(lf, tri, eye)


def _bias_tables_kernel(rb_ref, bown_ref, bprev_ref, b31_ref, bnew_ref,
                        town_ref, tprev_ref, t31_ref, tnew_ref, far_ref):
    h = pl.program_id(0)

    def lookup(bucket):
        out = jnp.full(bucket.shape, NEG, F32)
        for b in range(REL_BUCKETS):
            out = jnp.where(bucket == b, rb_ref[b, h] * LOG2E, out)
        return out

    town_ref[0] = lookup(bown_ref[...])
    tprev_ref[0] = lookup(bprev_ref[...])
    t31_ref[0] = lookup(b31_ref[...])
    tnew_ref[0] = lookup(bnew_ref[...])
    far_ref[0] = jnp.full(far_ref.shape[1:], rb_ref[REL_BUCKETS - 1, h] * LOG2E, F32)


def _bias_tables(rel_bias, bown, bprev, b31, bnew):
    n_heads = rel_bias.shape[1]
    full = lambda a: pl.BlockSpec(a.shape, lambda h: (0,) * a.ndim)
    per_head = lambda a: pl.BlockSpec((1,) + a.shape, lambda h: (h,) + (0,) * a.ndim)
    shp = lambda a: jax.ShapeDtypeStruct((n_heads,) + a.shape, F32)
    far = jax.ShapeDtypeStruct((n_heads, 8, LANES), F32)
    return pl.pallas_call(
        _bias_tables_kernel, grid=(n_heads,),
        in_specs=[pl.BlockSpec(memory_space=pltpu.SMEM), full(bown), full(bprev), full(b31), full(bnew)],
        out_specs=(per_head(bown), per_head(bprev), per_head(b31), per_head(bnew),
                   pl.BlockSpec((1, 8, LANES), lambda h: (h, 0, 0))),
        out_shape=(shp(bown), shp(bprev), shp(b31), shp(bnew), far),
        compiler_params=_cp(("parallel",)), name="bias_tables",
    )(rel_bias, bown, bprev, b31, bnew)


def _flash_kernel(*refs, mode, t):
    s_sc = refs[-2:]
    refs = refs[:-2]
    if mode == "mla":
        q_ref, k_ref, v_ref, o_ref, m_sc, acc_sc = refs
    elif mode == "fox":
        q_ref, k_ref, v_ref, cum_ref, cumt_ref, o_ref, m_sc, acc_sc = refs
    else:
        q_ref, k_ref, v_ref, kmean_ref, town_ref, tprev_ref, far_ref, o_ref, m_sc, acc_sc = refs
    hp = pl.program_id(1)
    qi = pl.program_id(2)
    lane = lax.broadcasted_iota(jnp.int32, (t, LANES), 1)
    upper = lane >= 64
    if mode == "mla":
        vhalf = (0, 1)
    else:
        khalf = (hp // 2) % 2
        vhalf = (khalf, khalf)

    q = []
    for a in range(2):
        if mode == "mla":
            q.append(q_ref[:, LANES * a:LANES * (a + 1)])
        else:
            qb = q_ref[...]
            src = jnp.where(khalf == a, qb, pltpu.roll(qb, 64, 1))
            q.append(jnp.where(upper.astype(jnp.int32) == khalf, src, jnp.zeros_like(src)))
    m_sc[...] = jnp.full(m_sc.shape, NEG, F32)
    acc_sc[...] = jnp.zeros(acc_sc.shape, F32)

    if mode == "fox":
        lane_c = lax.broadcasted_iota(jnp.int32, cum_ref.shape, 1)
        qc = [jnp.sum(jnp.where(lane_c == 2 * hp + a, cum_ref[...], 0.0), axis=-1, keepdims=True) * LOG2E
              for a in range(2)]
    if mode == "moba":
        n_sub = 16
        assert k_ref.shape[0] // MOBA_BLOCK <= n_sub
        km = kmean_ref[0, 0:n_sub, :]
        blk = lax.broadcasted_iota(jnp.int32, (n_sub, t), 0)
        sel_t = []
        for a in range(2):
            gate_t = sum(_dot_nt(p, q[a]) for p in _split(km, 2))
            selected = _top_blocks(gate_t, blk, qi, MOBA_TOPK, axis=0)
            sel_t.append(jnp.where(selected, 0.0, NEG).astype(BF16))
        row_b = lax.broadcasted_iota(jnp.int32, (n_sub, LANES), 0)

    row_i = lax.broadcasted_iota(jnp.int32, (t, t), 0)
    col_i = lax.broadcasted_iota(jnp.int32, (t, t), 1)
    causal = row_i >= col_i

    def scores(j, slot):
        rows = pl.ds(pl.multiple_of(j * t, t), t)
        for a in range(2):
            k = k_ref[rows, LANES * a:LANES * (a + 1)] if mode == "mla" else k_ref[rows, :]
            s_sc[slot][a] = _dot_nt(q[a], k)

    def attend(j, slot, kind):
        rows = pl.ds(pl.multiple_of(j * t, t), t)
        v = v_ref[rows, :]
        ones = jnp.ones_like(v)
        if mode == "moba":
            onehot = (row_b == j).astype(BF16)
        for a in range(2):
            s = s_sc[slot][a]
            shift = None
            if mode == "fox":
                s = s - cumt_ref[0, pl.ds(2 * hp + a, 1), rows] * LOG2E
                shift = qc[a]
            if mode == "moba":
                if kind == "diag":
                    s = s + town_ref[a]
                else:
                    sel = _dot_tn(sel_t[a], onehot)
                    if kind == "prev":
                        s = s + tprev_ref[a] + jnp.concatenate([sel] * (t // LANES), axis=1)
                    else:
                        sel = sel + far_ref[a, 0:1, :]
                        s = s + jnp.concatenate([sel] * (t // LANES), axis=1)
            elif kind == "diag":
                s = jnp.where(causal, s, NEG)
            m_old = m_sc[a]
            row_max = jnp.max(s, axis=-1, keepdims=True)
            if shift is not None:
                m_new = jnp.maximum(m_old, row_max + shift)
                p = jnp.exp2(s - (m_new - shift))
            else:
                m_new = jnp.maximum(m_old, row_max)
                p = jnp.exp2(s - m_new)
            alpha = jnp.exp2(m_old - m_new)
            if mode == "mla":
                v_a = jnp.where(upper, v, ones) if a else jnp.where(upper, ones, v)
            else:
                v_a = jnp.where(upper.astype(jnp.int32) == khalf, v, ones)
            acc_sc[a] = alpha * acc_sc[a] + jnp.dot(p.astype(BF16), v_a, preferred_element_type=F32)
            m_sc[a] = m_new

    def loop_body(i, carry):
        scores(2 * i + 1, 1)
        attend(2 * i, 0, "off")
        scores(2 * i + 2, 0)
        attend(2 * i + 1, 1, "off")
        return carry

    n_off = jnp.maximum(qi - 1, 0) if mode == "moba" else qi
    scores(0, 0)
    lax.fori_loop(0, n_off // 2, loop_body, 0)
    first = 2 * (n_off // 2)
    if mode == "moba":
        @pl.when(qi == 0)
        def _():
            attend(0, 0, "diag")

        @pl.when(jnp.logical_and(qi >= 1, first == qi - 1))
        def _():
            scores(qi, 1)
            attend(qi - 1, 0, "prev")
            attend(qi, 1, "diag")

        @pl.when(jnp.logical_and(qi >= 1, first == qi - 2))
        def _():
            scores(qi - 1, 1)
            attend(qi - 2, 0, "off")
            scores(qi, 0)
            attend(qi - 1, 1, "prev")
            attend(qi, 0, "diag")
    else:
        @pl.when(first == qi)
        def _():
            attend(qi, 0, "diag")

        @pl.when(first == qi - 1)
        def _():
            scores(qi, 1)
            attend(qi - 1, 0, "off")
            attend(qi, 1, "diag")

    outs = []
    for a in range(2):
        acc = acc_sc[a]
        o = acc * (1.0 / pltpu.roll(acc, 64, 1))
        if mode != "mla":
            o = jnp.where(khalf == a, o, pltpu.roll(o, 64, 1))
        outs.append(o)
    o_ref[...] = jnp.where(upper, outs[1], outs[0]).astype(o_ref.dtype)


def _flash(mode, q, k, v, extra, *, b, s, t):
    nq = s // t
    if mode == "mla":
        n_pairs = v.shape[1] // LANES
        qspec = pl.BlockSpec((t, 2 * LANES), lambda bi, hp, qi: (bi * nq + qi, hp))
        kspec = pl.BlockSpec((s, 2 * LANES), lambda bi, hp, qi: (bi, hp))
        vspec = pl.BlockSpec((s, LANES), lambda bi, hp, qi: (bi, hp))
        extra_specs = []
    else:
        n_pairs = q.shape[1] // LANES
        group_pairs = n_pairs // (k.shape[1] // HEAD_DIM)
        qspec = pl.BlockSpec((t, LANES), lambda bi, hp, qi: (bi * nq + qi, hp))
        kspec = pl.BlockSpec((s, LANES), lambda bi, hp, qi: (bi, hp // (2 * group_pairs)))
        vspec = kspec
        if mode == "fox":
            cum, cumt = extra
            extra_specs = [pl.BlockSpec((None, t, cum.shape[2]), lambda bi, hp, qi: (bi, qi, 0)),
                           pl.BlockSpec((1, cumt.shape[1], s), lambda bi, hp, qi: (bi, 0, 0))]
        else:
            kmean, town, tprev, far = extra
            extra_specs = [pl.BlockSpec((1,) + kmean.shape[1:], lambda bi, hp, qi: (bi, 0, 0)),
                           pl.BlockSpec((2, t, t), lambda bi, hp, qi: (hp, 0, 0)),
                           pl.BlockSpec((2, t, t), lambda bi, hp, qi: (hp, 0, 0)),
                           pl.BlockSpec((2,) + far.shape[1:], lambda bi, hp, qi: (hp, 0, 0))]
    return pl.pallas_call(
        functools.partial(_flash_kernel, mode=mode, t=t),
        grid=(b, n_pairs, nq),
        in_specs=[qspec, kspec, vspec] + extra_specs,
        out_specs=pl.BlockSpec((t, LANES), lambda bi, hp, qi: (bi * nq + qi, hp)),
        out_shape=jax.ShapeDtypeStruct((b * s, n_pairs * LANES), BF16),
        scratch_shapes=[pltpu.VMEM((2, t, 1), F32), pltpu.VMEM((2, t, LANES), F32),
                        pltpu.VMEM((2, t, t), F32), pltpu.VMEM((2, t, t), F32)],
        compiler_params=_cp(("parallel", "parallel", "arbitrary")), name="flash_" + mode,
    )(q, k, v, *extra)


def _page_specs(layer, n_chunk_pages, shape, order):
    specs = []
    for i in range(n_chunk_pages):
        specs.append(pl.BlockSpec(
            (None, None) + shape,
            lambda b, j, pt, i=i: (layer, pt[b, order(j) * n_chunk_pages + i], 0, 0)))
    return specs


def _online_softmax(s, m_sc, l_sc, acc_sc, pv):
    m_old = m_sc[...]
    m_new = jnp.maximum(m_old, jnp.max(s, axis=-1, keepdims=True))
    alpha = jnp.exp2(m_old - m_new)
    p = jnp.exp2(s - m_new)
    l_sc[...] = alpha * l_sc[...] + jnp.sum(p, axis=-1, keepdims=True)
    acc_sc[...] = alpha * acc_sc[...] + pv(p)
    m_sc[...] = m_new


def _own_head_lanes(out, rows_per_kv):
    row = lax.broadcasted_iota(jnp.int32, out.shape, 0)
    lane = lax.broadcasted_iota(jnp.int32, out.shape, 1)
    y = jnp.where(lane // HEAD_DIM == row // rows_per_kv, out, 0.0)
    while y.shape[1] > LANES:
        half = y.shape[1] // 2
        y = y[:, :half] + y[:, half:]
    return y + pltpu.roll(y, 64, 1)


def _mla_sample_kernel(pt_ref, *refs, cp, n_chunks, qk_dim, nope, scale):
    c_refs = refs[:cp]
    kr_refs = refs[cp:2 * cp]
    (qbd_ref, qr_ref, cnew_ref, krnew_ref, cosp_ref, sinp_ref, cosn_ref, sinn_ref, wukt_ref, wuv_ref,
     gkr_ref, mask_ref, o_ref, m_sc, l_sc, acc_sc) = refs[2 * cp:]
    j = pl.program_id(1)
    tn = cnew_ref.shape[1]
    n_heads = wuv_ref.shape[0]

    @pl.when(j == 0)
    def _():
        m_sc[...] = jnp.full(m_sc.shape, NEG, F32)
        l_sc[...] = jnp.zeros(l_sc.shape, F32)
        acc_sc[...] = jnp.zeros(acc_sc.shape, F32)

    def process(c, krt, cos_t, sin_t, mask):
        cb = c.astype(BF16)
        knt = _dot_nt(wukt_ref[...], cb)
        sq = knt * knt
        ssr = jnp.sum(krt * krt, axis=0, keepdims=True)
        inv = []
        for hh in range(n_heads):
            ss = jnp.sum(sq[nope * hh:nope * (hh + 1)], axis=0, keepdims=True) + ssr
            inv.append(lax.rsqrt(ss * (1.0 / qk_dim) + EPS) * scale)
        inv = _repeat_rows(jnp.concatenate(inv, axis=0), tn)
        krg = krt * gkr_ref[...]
        half = krt.shape[0] // 2
        krr = krg * cos_t + jnp.concatenate([krg[half:], krg[:half]], axis=0) * sin_t
        s = (_dot(qbd_ref[0], knt) + _dot(qr_ref[0], krr)) * inv
        if mask is not None:
            s = jnp.where(mask, s, NEG)
        _online_softmax(s, m_sc, l_sc, acc_sc, lambda p: _dot(p, cb))

    c = jnp.concatenate([r[...] for r in c_refs], axis=0)
    krt = jnp.concatenate([r[...] for r in kr_refs], axis=1)
    process(c, krt, cosp_ref[...], sinp_ref[...], None)

    @pl.when(j == n_chunks - 1)
    def _():
        process(cnew_ref[0], krnew_ref[0], cosn_ref[...], sinn_ref[...], mask_ref[...] > 0.5)
        pc = acc_sc[...] * (1.0 / l_sc[...])
        for hh in range(n_heads):
            o_ref[0, tn * hh:tn * (hh + 1), :] = _dot(pc[tn * hh:tn * (hh + 1), :], wuv_ref[hh])


def _mla_sample(page_table, cache_c, cache_krt, layer, qbd, qr, c_new, krt_new, cos_p, sin_p, cos_n, sin_n,
                wukt, wuv, gkr, mask, *, qk_dim, nope, scale):
    db, n_pages = page_table.shape
    cp = min(CHUNK_PAGES, n_pages)
    n_chunks = n_pages // cp
    tn = c_new.shape[1]
    n_heads = wuv.shape[0]
    rows = n_heads * tn
    fwd = lambda j: j
    per_seq = lambda a: pl.BlockSpec((1,) + a.shape[1:], lambda b, j, pt: (b,) + (0,) * (a.ndim - 1))
    full = lambda a: pl.BlockSpec(a.shape, lambda b, j, pt: (0,) * a.ndim)
    chunk_cols = lambda a: pl.BlockSpec((a.shape[0], cp * 128), lambda b, j, pt: (0, j))
    in_specs = (_page_specs(layer, cp, cache_c.shape[2:], fwd) + _page_specs(layer, cp, cache_krt.shape[2:], fwd)
                + [per_seq(qbd), per_seq(qr), per_seq(c_new), per_seq(krt_new), chunk_cols(cos_p), chunk_cols(sin_p),
                   full(cos_n), full(sin_n), full(wukt), full(wuv), full(gkr), full(mask)])
    grid_spec = pltpu.PrefetchScalarGridSpec(
        num_scalar_prefetch=1, grid=(db, n_chunks), in_specs=in_specs,
        out_specs=pl.BlockSpec((1, rows, wuv.shape[2]), lambda b, j, pt: (b, 0, 0)),
        scratch_shapes=[pltpu.VMEM((rows, 1), F32), pltpu.VMEM((rows, 1), F32), pltpu.VMEM((rows, cache_c.shape[-1]), F32)])
    return pl.pallas_call(
        functools.partial(_mla_sample_kernel, cp=cp, n_chunks=n_chunks, qk_dim=qk_dim, nope=nope, scale=scale),
        grid_spec=grid_spec,
        out_shape=jax.ShapeDtypeStruct((db, rows, wuv.shape[2]), F32),
        compiler_params=_cp(("parallel", "arbitrary")), name="mla_sample",
    )(page_table, *([cache_c] * cp), *([cache_krt] * cp), qbd, qr, c_new, krt_new, cos_p, sin_p, cos_n, sin_n,
      wukt, wuv, gkr, mask)


def _moba_sample_kernel(pt_ref, *refs, cp, n_chunks, n_blocks, rows_per_kv):
    k_refs = refs[:cp]
    v_refs = refs[cp:2 * cp]
    (q_ref, knew_ref, vnew_ref, t31_ref, tnew_ref, far_ref, ind_ref, expand_ref, o_ref,
     kt_all, vt_all, g_all) = refs[2 * cp:]
    j = pl.program_id(1)
    q = q_ref[0]
    bpc = cp * 128 // MOBA_BLOCK
    tk = cp * 128

    kt = jnp.concatenate([r[...] for r in k_refs], axis=1)
    vt = jnp.concatenate([r[...] for r in v_refs], axis=1)
    cols = pl.ds(pl.multiple_of(j * tk, tk), tk)
    kt_all[:, cols] = kt.astype(BF16)
    vt_all[:, cols] = vt.astype(BF16)
    kmean = _dot_exact_rhs(kt, ind_ref[...], 3) * (1.0 / MOBA_BLOCK)
    gate = pltpu.roll(_dot_exact_lhs(q, kmean, 2), j * bpc, 1)

    @pl.when(j == 0)
    def _():
        g_all[...] = gate

    @pl.when(j > 0)
    def _():
        g_all[...] += gate

    @pl.when(j == n_chunks - 1)
    def _():
        lane = lax.broadcasted_iota(jnp.int32, g_all.shape, 1)
        picked = _top_blocks(g_all[...], lane, n_blocks, min(MOBA_TOPK, n_blocks + 1), axis=1)
        selb = jnp.where(picked, 0.0, NEG).astype(BF16)
        s = _dot(q, kt_all[...]) + jnp.dot(selb, expand_ref[...], preferred_element_type=F32)
        last = s.shape[1] - MOBA_BLOCK
        s = jnp.concatenate([s[:, :last] + far_ref[...], s[:, last:] + t31_ref[...]], axis=1)
        s_new = _dot_nt(q, knew_ref[0]) + tnew_ref[...]
        m = jnp.maximum(jnp.max(s, axis=-1, keepdims=True), jnp.max(s_new, axis=-1, keepdims=True))
        p = jnp.exp2(s - m)
        p_new = jnp.exp2(s_new - m)
        l = jnp.sum(p, axis=-1, keepdims=True) + jnp.sum(p_new, axis=-1, keepdims=True)
        out = (_dot_nt(p, vt_all[...]) + _dot(p_new, vnew_ref[0])) * (1.0 / l)
        o_ref[0] = _own_head_lanes(out, rows_per_kv)


def _moba_sample(page_table, cache_kt, cache_vt, layer, q, k_new, v_new, t31, tnew, far, *, rows_per_kv):
    db, n_pages = page_table.shape
    cp = min(CHUNK_PAGES, n_pages)
    n_chunks = n_pages // cp
    past = n_pages * 128
    n_blocks = past // MOBA_BLOCK
    assert n_blocks <= LANES
    rows = q.shape[1]
    key_blk = jnp.arange(past) // MOBA_BLOCK
    ind = (key_blk[:cp * 128, None] == jnp.arange(LANES)[None, :]).astype(BF16)
    expand = (jnp.arange(LANES)[:, None] == key_blk[None, :]).astype(BF16)
    fwd = lambda j: j
    per_seq = lambda a: pl.BlockSpec((1,) + a.shape[1:], lambda b, j, pt: (b,) + (0,) * (a.ndim - 1))
    full = lambda a: pl.BlockSpec(a.shape, lambda b, j, pt: (0,) * a.ndim)
    in_specs = (_page_specs(layer, cp, cache_kt.shape[2:], fwd) + _page_specs(layer, cp, cache_vt.shape[2:], fwd)
                + [per_seq(q), per_seq(k_new), per_seq(v_new), full(t31), full(tnew), full(far), full(ind), full(expand)])
    grid_spec = pltpu.PrefetchScalarGridSpec(
        num_scalar_prefetch=1, grid=(db, n_chunks), in_specs=in_specs,
        out_specs=pl.BlockSpec((1, rows, LANES), lambda b, j, pt: (b, 0, 0)),
        scratch_shapes=[pltpu.VMEM((cache_kt.shape[2], past), BF16), pltpu.VMEM((cache_vt.shape[2], past), BF16),
                        pltpu.VMEM((rows, LANES), F32)])
    return pl.pallas_call(
        functools.partial(_moba_sample_kernel, cp=cp, n_chunks=n_chunks, n_blocks=n_blocks, rows_per_kv=rows_per_kv),
        grid_spec=grid_spec, out_shape=jax.ShapeDtypeStruct((db, rows, LANES), F32),
        compiler_params=_cp(("parallel", "arbitrary")), name="moba_sample",
    )(page_table, *([cache_kt] * cp), *([cache_vt] * cp), q, k_new, v_new, t31, tnew, far, ind, expand)


def _fox_sample_kernel(pt_ref, *refs, cp, n_chunks, rows_per_kv):
    k_refs = refs[:cp]
    v_refs = refs[cp:2 * cp]
    lf_refs = refs[2 * cp:3 * cp]
    (q_ref, knew_ref, vnew_ref, lftnew_ref, rep_ref, tri_ref, tmask_ref, cmask_ref, o_ref,
     m_sc, l_sc, acc_sc, carry_sc, ncol_sc) = refs[3 * cp:]
    j = pl.program_id(1)
    q = q_ref[0]
    n_h = lftnew_ref.shape[1]
    reps = q.shape[0] // n_h

    @pl.when(j == 0)
    def _():
        m_sc[...] = jnp.full(m_sc.shape, NEG, F32)
        l_sc[...] = jnp.zeros(l_sc.shape, F32)
        acc_sc[...] = jnp.zeros(acc_sc.shape, F32)
        carry_sc[...] = jnp.zeros(carry_sc.shape, F32)
        n_t = _dot_exact_rhs(lftnew_ref[0] * LOG2E, tri_ref[...], 3)
        nmat = _dot_exact_lhs(rep_ref[...], n_t, 3)
        ncol = jnp.sum(nmat * tmask_ref[...], axis=-1, keepdims=True)
        ncol_sc[...] = ncol
        s = _dot_nt(q, knew_ref[0]) + (ncol - nmat)
        _online_softmax(jnp.where(cmask_ref[...] > 0.5, s, NEG), m_sc, l_sc, acc_sc, lambda p: _dot(p, vnew_ref[0]))

    kt = jnp.concatenate([r[...] for r in k_refs], axis=1)
    vt = jnp.concatenate([r[...] for r in v_refs], axis=1)
    lft = jnp.concatenate([r[...] for r in lf_refs], axis=1) * LOG2E
    tk = lft.shape[1]
    lane = lax.broadcasted_iota(jnp.int32, lft.shape, 1)
    x = lft
    sh = 1
    while sh < tk:
        if sh % LANES == 0:
            shifted = jnp.concatenate([x[:, sh:], jnp.zeros((n_h, sh), F32)], axis=1)
        else:
            shifted = jnp.where(lane < tk - sh, pltpu.roll(x, tk - sh, 1), 0.0)
        x = x + shifted
        sh *= 2
    carry = carry_sc[...]
    later = x - lft + carry
    carry_sc[...] = carry + x[:, 0:1]
    s = _dot(q, kt) + _repeat_rows(later, reps) + ncol_sc[...]
    _online_softmax(s, m_sc, l_sc, acc_sc, lambda p: _dot_nt(p, vt))

    @pl.when(j == n_chunks - 1)
    def _():
        o_ref[0] = _own_head_lanes(acc_sc[...] * (1.0 / l_sc[...]), rows_per_kv)


def _fox_sample(page_table, cache_kt, cache_vt, cache_lft, layer, q, k_new, v_new, lft_new, rep, tri, tmask, cmask,
                *, rows_per_kv):
    db, n_pages = page_table.shape
    cp = min(CHUNK_PAGES, n_pages)
    n_chunks = n_pages // cp
    rows = q.shape[1]
    rev = lambda j: n_chunks - 1 - j
    per_seq = lambda a: pl.BlockSpec((1,) + a.shape[1:], lambda b, j, pt: (b,) + (0,) * (a.ndim - 1))
    full = lambda a: pl.BlockSpec(a.shape, lambda b, j, pt: (0,) * a.ndim)
    in_specs = (_page_specs(layer, cp, cache_kt.shape[2:], rev) + _page_specs(layer, cp, cache_vt.shape[2:], rev)
                + _page_specs(layer, cp, cache_lft.shape[2:], rev)
                + [per_seq(q), per_seq(k_new), per_seq(v_new), per_seq(lft_new), full(rep), full(tri), full(tmask), full(cmask)])
    grid_spec = pltpu.PrefetchScalarGridSpec(
        num_scalar_prefetch=1, grid=(db, n_chunks), in_specs=in_specs,
        out_specs=pl.BlockSpec((1, rows, LANES), lambda b, j, pt: (b, 0, 0)),
        scratch_shapes=[pltpu.VMEM((rows, 1), F32), pltpu.VMEM((rows, 1), F32), pltpu.VMEM((rows, q.shape[2]), F32),
                        pltpu.VMEM((lft_new.shape[1], 1), F32), pltpu.VMEM((rows, 1), F32)])
    return pl.pallas_call(
        functools.partial(_fox_sample_kernel, cp=cp, n_chunks=n_chunks, rows_per_kv=rows_per_kv),
        grid_spec=grid_spec, out_shape=jax.ShapeDtypeStruct((db, rows, LANES), F32),
        compiler_params=_cp(("parallel", "arbitrary")), name="fox_sample",
    )(page_table, *([cache_kt] * cp), *([cache_vt] * cp), *([cache_lft] * cp), q, k_new, v_new, lft_new,
      rep, tri, tmask, cmask)


def _mix_mlp_kernel(x_ref, o_ref, wo_ref, g_ref, up_ref, down_ref, y_ref, x1_sc, h_sc, acc_sc):
    f = pl.program_id(1)

    @pl.when(f == 0)
    def _():
        x1 = x_ref[...] + jnp.dot(o_ref[...], wo_ref[...], preferred_element_type=F32)
        x1_sc[...] = x1
        h_sc[...] = _rms(x1, g_ref[...]).astype(BF16)
        acc_sc[...] = jnp.zeros(acc_sc.shape, F32)

    u = jnp.maximum(jnp.dot(h_sc[...], up_ref[...], preferred_element_type=F32), 0.0)
    acc_sc[...] += jnp.dot((u * u).astype(BF16), down_ref[...], preferred_element_type=F32)

    @pl.when(f == pl.num_programs(1) - 1)
    def _():
        y_ref[...] = x1_sc[...] + acc_sc[...]


def _mix_mlp(x, o, wo, g, up, down, *, tm):
    n, d = x.shape
    dff = up.shape[1]
    tf = min(FF_TILE, dff)
    return pl.pallas_call(
        _mix_mlp_kernel, grid=(n // tm, dff // tf),
        in_specs=[pl.BlockSpec((tm, d), lambda i, f: (i, 0)), pl.BlockSpec((tm, o.shape[1]), lambda i, f: (i, 0)),
                  pl.BlockSpec(wo.shape, lambda i, f: (0, 0)), pl.BlockSpec(g.shape, lambda i, f: (0, 0)),
                  pl.BlockSpec((d, tf), lambda i, f: (0, f)), pl.BlockSpec((tf, d), lambda i, f: (f, 0))],
        out_specs=pl.BlockSpec((tm, d), lambda i, f: (i, 0)),
        out_shape=jax.ShapeDtypeStruct((n, d), F32),
        scratch_shapes=[pltpu.VMEM((tm, d), F32), pltpu.VMEM((tm, d), BF16), pltpu.VMEM((tm, d), F32)],
        compiler_params=_cp(("parallel", "arbitrary")), name="mix_mlp",
    )(x, o, wo, g, up, down)


def _t5_bucket(rel):
    n = jnp.maximum(rel, 0)
    exact = REL_BUCKETS // 2
    scaled = jnp.log(jnp.maximum(n, 1).astype(F32) / exact) / math.log(REL_MAX_DIST / exact)
    large = exact + (scaled * (REL_BUCKETS - exact)).astype(jnp.int32)
    return jnp.where(n < exact, n, jnp.minimum(large, REL_BUCKETS - 1))


def _rope_tables(pos, half):
    inv = ROPE_THETA ** (-jnp.arange(half, dtype=F32) / half)
    ang = pos.astype(F32)[:, None] * inv[None, :]
    return jnp.cos(ang), jnp.sin(ang)


def _pick_tile(n, cap):
    t = cap
    while n % t:
        t //= 2
    return t


def _pages_on_lanes(cache):
    nd = cache.ndim
    t = jnp.transpose(cache, (0, 1) + tuple(range(3, nd)) + (2,))
    return t.reshape(t.shape[:2] + (-1, t.shape[-1]))


def kernel(x_prompt, x_sample, cache_mla_latent, cache_mla_krope, cache_moba_k, cache_moba_v, cache_fox_k, cache_fox_v, cache_fox_logf, page_table, rel_bias, g_attn, g_mlp, w_mlp_up, w_mlp_down, w_in_even, g_q_lat, g_kv_lat, w_uq, w_uk, w_uv, g_mla_q, g_mla_k, g_moba_q, g_moba_k, w_o_even, w_in_odd, b_forget, g_fox_q, g_fox_k, w_o_odd):
    B, S, D = x_prompt.shape
    DB, TN, _ = x_sample.shape
    depth = g_attn.shape[0]
    n_pages = page_table.shape[1]
    page = cache_mla_latent.shape[2]
    past = n_pages * page
    n_p, n_s = B * S, DB * TN
    n = n_p + n_s
    q_lora, kv_lora = g_q_lat.shape[1], g_kv_lat.shape[1]
    mla_heads, nope, mla_v = w_uk.shape[2], w_uk.shape[3], w_uv.shape[3]
    qk_dim = w_uq.shape[3]
    rope_d = qk_dim - nope
    half = rope_d // 2
    moba_kv = cache_moba_k.shape[3]
    moba_heads = rel_bias.shape[1]
    moba_g = moba_heads // moba_kv
    fox_heads = b_forget.shape[1]
    fox_kv = cache_fox_k.shape[3]
    fox_g = fox_heads // fox_kv
    mla_scale = float(qk_dim) ** -0.5
    att_scale = HEAD_DIM ** -0.5
    assert (q_lora, kv_lora, mla_heads, nope, rope_d, mla_v) == (256, 256, 8, 64, 32, 64)
    assert (moba_heads, moba_kv, fox_heads, fox_kv, page, TN) == (8, 2, 16, 4, 128, 8)
    assert past % MOBA_BLOCK == 0 and S % 512 == 0 and n_p % 256 == 0 and n_s % 256 == 0
    assert MOBA_BLOCK + 1 >= REL_MAX_DIST
    tm = _pick_tile(math.gcd(n_p, n_s), TOKEN_TILE)

    pos = jnp.concatenate([jnp.tile(jnp.arange(S, dtype=jnp.int32), B),
                           jnp.tile(past + jnp.arange(TN, dtype=jnp.int32), DB)])
    cos, sin = _rope_tables(pos, half)
    ones64 = jnp.ones((n, 64), F32)
    zeros32 = jnp.zeros((n, 32), F32)
    cos_t = jnp.concatenate([ones64, cos, cos, zeros32], axis=1)
    sin_t = jnp.concatenate([jnp.zeros((n, 64), F32), -sin, sin, zeros32], axis=1)
    kcos, ksin = _rope_tables(jnp.arange(past + TN, dtype=jnp.int32), half)
    kcos_t = jnp.concatenate([kcos, kcos], axis=1).T
    ksin_t = jnp.concatenate([-ksin, ksin], axis=1).T

    bd = (jnp.arange(512)[:, None] // HEAD_DIM == jnp.arange(512)[None, :] // HEAD_DIM).astype(BF16)
    t_new = jnp.arange(TN)
    t_mla = jnp.arange(mla_heads * TN) % TN
    mla_mask = (t_new[None, :] <= t_mla[:, None]).astype(F32)
    r_fox = jnp.arange(fox_heads * TN)
    rep_fox = (r_fox[:, None] // TN == jnp.arange(fox_heads)[None, :]).astype(BF16)
    tri_new = (t_new[:, None] <= t_new[None, :]).astype(BF16)
    tmask_fox = (t_new[None, :] == (r_fox % TN)[:, None]).astype(F32)
    cmask_fox = (t_new[None, :] <= (r_fox % TN)[:, None]).astype(F32)

    ii = jnp.arange(MOBA_BLOCK)
    rel_own = ii[:, None] - ii[None, :]
    b_own = jnp.where(rel_own >= 0, _t5_bucket(rel_own), -1)
    b_prev = _t5_bucket(rel_own + MOBA_BLOCK)
    b_31 = _t5_bucket(MOBA_BLOCK + t_new[:, None] - ii[None, :])
    rel_new = t_new[:, None] - t_new[None, :]
    b_new = jnp.where(rel_new >= 0, _t5_bucket(rel_new), -1)
    b_new = jnp.pad(b_new, ((0, 0), (0, LANES - TN)), constant_values=-1)
    town, tprev, t31, tnew, far = _bias_tables(rel_bias.astype(F32), b_own.astype(jnp.int32), b_prev.astype(jnp.int32),
                                               b_31.astype(jnp.int32), b_new.astype(jnp.int32))
    t31_s = t31.reshape(moba_heads * TN, MOBA_BLOCK)
    tnew_s = tnew.reshape(moba_heads * TN, LANES)[:, :TN]
    far_s = far.reshape(moba_heads * 8, LANES)[:, :1]

    krope_t = _pages_on_lanes(cache_mla_krope)
    moba_kt, moba_vt = _pages_on_lanes(cache_moba_k), _pages_on_lanes(cache_moba_v)
    fox_kt, fox_vt = _pages_on_lanes(cache_fox_k), _pages_on_lanes(cache_fox_v)
    fox_lft = _pages_on_lanes(cache_fox_logf)

    pt = page_table.astype(jnp.int32)
    x = jnp.concatenate([x_prompt.reshape(n_p, D), x_sample.reshape(n_s, D)], axis=0)
    outs = {k: [] for k in ("lat", "kr", "mk", "mv", "fk", "fv", "lf")}

    for layer in range(depth):
        ga = g_attn[layer][None, :]
        if layer % 2 == 0:
            e = layer // 2
            w = w_in_even[e]
            wp = jnp.concatenate([w[:, 0:512], w[:, 544:1312], jnp.zeros((D, nope), F32), w[:, 512:544],
                                  jnp.zeros((D, LANES - qk_dim), F32)], axis=1).astype(BF16)
            wuq = jnp.pad(w_uq[e], ((0, 0), (0, 0), (0, LANES - qk_dim))).reshape(q_lora, mla_heads * LANES).astype(BF16)
            wuk_pad = jnp.pad(w_uk[e], ((0, 0), (0, 0), (0, LANES - nope))).reshape(kv_lora, mla_heads * LANES).astype(BF16)
            wukt = w_uk[e].transpose(1, 2, 0).reshape(mla_heads * nope, kv_lora).astype(BF16)
            wuv = w_uv[e].reshape(kv_lora, mla_heads * mla_v).astype(BF16)
            wuv_h = w_uv[e].transpose(1, 0, 2).astype(BF16)
            gq = jnp.pad(g_mla_q[e], (0, LANES - qk_dim))[None, :]
            gkn = jnp.concatenate([g_mla_k[e][:nope], jnp.ones((LANES - nope,), F32)])[None, :]
            gkr = jnp.concatenate([jnp.zeros((nope,), F32), g_mla_k[e][nope:], jnp.zeros((LANES - qk_dim,), F32)])[None, :]
            gmq = (jnp.tile(g_moba_q[e], moba_heads) * (att_scale * LOG2E))[None, :]
            gmk = jnp.tile(g_moba_k[e], moba_kv)[None, :]
            (c, krb, mk, mv, qmla, kmla, vmla, qmo, kmo, vmo, kmean) = _even_proj(
                x, ga, wp, g_q_lat[e][None, :], g_kv_lat[e][None, :], wuq, gq, gkn, gkr, wuk_pad, wuv, gmq, gmk, bd,
                cos_t, sin_t, tm=tm, qk_dim=qk_dim, scale=mla_scale * LOG2E)
            kr = krb[:, 64:64 + rope_d]
            o1_p = _flash("mla", qmla, kmla, vmla, (), b=B, s=S, t=512)
            nblk = S // MOBA_BLOCK
            km_p = kmean.reshape(n // MOBA_BLOCK, 128)[:B * nblk].reshape(B, nblk, 128)
            km_p = jnp.pad(km_p, ((0, 0), (0, LANES - nblk), (0, 0)))
            o2_p = _flash("moba", qmo, kmo, vmo, (km_p, town, tprev, far), b=B, s=S, t=MOBA_BLOCK)
            qs = qmla[n_p:].reshape(DB, TN, mla_heads, LANES).transpose(0, 2, 1, 3)
            qbd = jnp.einsum('bhtd,hj->bhtjd', qs[..., :nope], jnp.eye(mla_heads, dtype=BF16))
            qbd = qbd.reshape(DB, mla_heads * TN, mla_heads * nope)
            qr = qs[..., nope:qk_dim].reshape(DB, mla_heads * TN, rope_d)
            o1_s = _mla_sample(pt, cache_mla_latent, krope_t, e, qbd, qr,
                               c[n_p:].reshape(DB, TN, kv_lora), kr[n_p:].reshape(DB, TN, rope_d).transpose(0, 2, 1),
                               kcos_t[:, :past], ksin_t[:, :past], kcos_t[:, past:], ksin_t[:, past:],
                               wukt, wuv_h, g_mla_k[e][nope:][:, None], mla_mask,
                               qk_dim=qk_dim, nope=nope, scale=mla_scale * LOG2E)
            o1_s = o1_s.reshape(DB, mla_heads, TN, mla_v).transpose(0, 2, 1, 3).reshape(n_s, mla_heads * mla_v)
            q2 = qmo[n_p:].reshape(DB, TN, moba_kv, moba_g, HEAD_DIM).transpose(0, 2, 3, 1, 4)
            q2 = jnp.einsum('bkgtd,kj->bkgtjd', q2, jnp.eye(moba_kv, dtype=BF16)).reshape(DB, moba_heads * TN, moba_kv * HEAD_DIM)
            o2_s = _moba_sample(pt, moba_kt, moba_vt, e, q2, mk[n_p:].reshape(DB, TN, -1), mv[n_p:].reshape(DB, TN, -1),
                                t31_s, tnew_s, far_s, rows_per_kv=moba_g * TN)
            o2_s = o2_s[:, :, :HEAD_DIM].reshape(DB, moba_heads, TN, HEAD_DIM).transpose(0, 2, 1, 3)
            o2_s = o2_s.reshape(n_s, moba_heads * HEAD_DIM)
            o = jnp.concatenate([jnp.concatenate([o1_p, o2_p], axis=1),
                                 jnp.concatenate([o1_s.astype(BF16), o2_s.astype(BF16)], axis=1)], axis=0)
            wo = w_o_even[e].astype(BF16)
            outs["lat"].append(c); outs["kr"].append(kr); outs["mk"].append(mk); outs["mv"].append(mv)
        else:
            jl = layer // 2
            w = w_in_odd[jl]
            wp = jnp.concatenate([w, jnp.zeros((D, LANES - fox_heads), F32)], axis=1).astype(BF16)
            bf = jnp.pad(b_forget[jl], (0, LANES - fox_heads))[None, :]
            gq = (jnp.tile(g_fox_q[jl], fox_heads) * (att_scale * LOG2E))[None, :]
            gk = jnp.tile(g_fox_k[jl], fox_kv)[None, :]
            q, k, v, lf, kb, vb = _odd_proj(x, ga, wp, bf, gq, gk, bd, tm=tm)
            cum, cumt = _prompt_cumsum(lf[:n_p].reshape(B, S, LANES))
            o_p = _flash("fox", q, kb, vb, (cum, cumt), b=B, s=S, t=512)
            qf = q[n_p:].reshape(DB, TN, fox_kv, fox_g, HEAD_DIM).transpose(0, 2, 3, 1, 4)
            qf = jnp.einsum('bkgtd,kj->bkgtjd', qf, jnp.eye(fox_kv, dtype=BF16)).reshape(DB, fox_heads * TN, fox_kv * HEAD_DIM)
            lfh = lf[:, :fox_heads]
            o_s = _fox_sample(pt, fox_kt, fox_vt, fox_lft, jl, qf,
                              k[n_p:].reshape(DB, TN, -1), v[n_p:].reshape(DB, TN, -1),
                              lfh[n_p:].reshape(DB, TN, fox_heads).transpose(0, 2, 1),
                              rep_fox, tri_new, tmask_fox, cmask_fox, rows_per_kv=fox_g * TN)
            o_s = o_s[:, :, :HEAD_DIM].reshape(DB, fox_heads, TN, HEAD_DIM).transpose(0, 2, 1, 3).reshape(n_s, fox_heads * HEAD_DIM)
            o = jnp.concatenate([o_p, o_s.astype(BF16)], axis=0)
            wo = w_o_odd[jl].astype(BF16)
            outs["fk"].append(k); outs["fv"].append(v); outs["lf"].append(lfh)
        x = _mix_mlp(x, o, wo, g_mlp[layer][None, :], w_mlp_up[layer].astype(BF16), w_mlp_down[layer].astype(BF16), tm=tm)

    def split(name, tail):
        a = jnp.stack(outs[name])
        return a[:, :n_p].reshape((a.shape[0], B, S) + tail), a[:, n_p:].reshape((a.shape[0], DB, TN) + tail)

    lat_p, lat_s = split("lat", (kv_lora,))
    kr_p, kr_s = split("kr", (rope_d,))
    mk_p, mk_s = split("mk", (moba_kv, HEAD_DIM))
    mv_p, mv_s = split("mv", (moba_kv, HEAD_DIM))
    fk_p, fk_s = split("fk", (fox_kv, HEAD_DIM))
    fv_p, fv_s = split("fv", (fox_kv, HEAD_DIM))
    lf_p, lf_s = split("lf", (fox_heads,))
    return (x[:n_p].reshape(B, S, D), x[n_p:].reshape(DB, TN, D), lat_p, lat_s, kr_p, kr_s, mk_p, mk_s, mv_p, mv_s,
            fk_p, fk_s, fv_p, fv_s, lf_p, lf_s)
```

```python
import functools
import math

import jax
import jax.numpy as jnp
from jax import lax
from jax.experimental import pallas as pl
from jax.experimental.pallas import tpu as pltpu

F32 = jnp.float32
BF16 = jnp.bfloat16
NEG = -1e30
EPS = 1e-6
LOG2E = math.log2(math.e)
ROPE_THETA = 10000.0
HEAD_DIM = 64
MOBA_BLOCK = 256
MOBA_TOPK = 3
REL_BUCKETS = 32
REL_MAX_DIST = 128
LANES = 128
TOKEN_TILE = 512
FF_TILE = 1024
CHUNK_PAGES = 16
SEQS_PER_STEP = 2
VMEM_LIMIT = 56 * 1024 * 1024


def _cp(sem, vmem=VMEM_LIMIT):
    return pltpu.CompilerParams(dimension_semantics=sem, vmem_limit_bytes=vmem)


def _dot(a, b):
    return jnp.dot(a.astype(BF16), b.astype(BF16), preferred_element_type=F32)


def _dot_nt(a, b):
    return lax.dot_general(a.astype(BF16), b.astype(BF16), (((1,), (1,)), ((), ())),
                           preferred_element_type=F32)


def _dot_tn(a, b):
    return lax.dot_general(a.astype(BF16), b.astype(BF16), (((0,), (0,)), ((), ())),
                           preferred_element_type=F32)


def _split(a, n):
    parts = []
    r = a.astype(F32)
    for _ in range(n):
        p = r.astype(BF16)
        parts.append(p)
        r = r - p.astype(F32)
    return parts


def _dot_exact_rhs(a, b, n=3):
    return sum(jnp.dot(p, b, preferred_element_type=F32) for p in _split(a, n))


def _dot_exact_lhs(a, b, n=3):
    return sum(jnp.dot(a, p, preferred_element_type=F32) for p in _split(b, n))


def _dot_nt_exact_lhs(a, b, n=3):
    return sum(lax.dot_general(a, p, (((1,), (1,)), ((), ())), preferred_element_type=F32)
               for p in _split(b, n))


def _rms(v, g):
    return v * lax.rsqrt(jnp.mean(v * v, axis=-1, keepdims=True) + EPS) * g


def _head_rms(v, g, bd, width):
    step = min(width, bd.shape[0])
    outs = []
    for s in range(0, width, step):
        blk = v[:, s:s + step]
        ss = _dot_exact_rhs(blk * blk, bd[:step, :step], 2)
        outs.append(blk * lax.rsqrt(ss * (1.0 / HEAD_DIM) + EPS))
    out = outs[0] if len(outs) == 1 else jnp.concatenate(outs, axis=-1)
    return out * g


def _rot_block(v, cos_t, sin_t):
    lane = lax.broadcasted_iota(jnp.int32, v.shape, 1)
    partner = jnp.where(lane < 80, pltpu.roll(v, LANES - 16, 1), pltpu.roll(v, 16, 1))
    return v * cos_t + partner * sin_t


def _repeat_rows(x, reps):
    return jnp.concatenate([jnp.broadcast_to(x[i:i + 1, :], (reps, x.shape[1])) for i in range(x.shape[0])], axis=0)


def _top_blocks(gate, blk, n_past, n_pick, axis):
    g = jnp.where(blk < n_past, gate, NEG)
    selected = jnp.zeros(gate.shape, jnp.bool_)
    for r in range(n_pick):
        mx = jnp.max(g, axis=axis, keepdims=True)
        idx = jnp.min(jnp.where(g == mx, blk, gate.shape[axis]), axis=axis, keepdims=True)
        hit = blk == idx
        selected = jnp.logical_or(selected, jnp.logical_and(hit, r < n_past))
        g = jnp.where(hit, -3e38, g)
    return selected


def _even_proj_kernel(x_ref, ga_ref, wp_ref, gql_ref, gkv_ref, wuq_ref, gq_ref, gkn_ref, gkr_ref,
                      wuk_ref, wuv_ref, gmq_ref, gmk_ref, bd_ref, cos_ref, sin_ref,
                      c_ref, krb_ref, mk_ref, mv_ref, qmla_ref, kmla_ref, vmla_ref,
                      qmo_ref, kmo_ref, vmo_ref, kmean_ref, *, n_heads, qk_dim, scale):
    x = x_ref[...]
    h = _rms(x, ga_ref[...]).astype(BF16)
    y = jnp.dot(h, wp_ref[...], preferred_element_type=F32)
    cos_t = cos_ref[...]
    sin_t = sin_ref[...]
    bd = bd_ref[...]

    c = _rms(y[:, 256:512], gkv_ref[...])
    c_ref[...] = c
    cb = c.astype(BF16)

    ql = _rms(y[:, 0:256], gql_ref[...]).astype(BF16)
    q = jnp.dot(ql, wuq_ref[...], preferred_element_type=F32)
    gq = gq_ref[...]
    gkn = gkn_ref[...]
    inv_d = 1.0 / qk_dim
    for hh in range(n_heads):
        sl = slice(LANES * hh, LANES * (hh + 1))
        qh = q[:, sl]
        qh = qh * lax.rsqrt(jnp.sum(qh * qh, axis=-1, keepdims=True) * inv_d + EPS) * gq
        qmla_ref[:, sl] = _rot_block(qh * gkn, cos_t, sin_t).astype(BF16)

    krb = y[:, 1280:1408]
    krb_ref[...] = krb
    ssr = jnp.sum(krb * krb, axis=-1, keepdims=True)
    krr = _rot_block(krb * gkr_ref[...], cos_t, sin_t)
    kn = jnp.dot(cb, wuk_ref[...], preferred_element_type=F32)
    for hh in range(n_heads):
        sl = slice(LANES * hh, LANES * (hh + 1))
        kh = kn[:, sl]
        inv = lax.rsqrt((jnp.sum(kh * kh, axis=-1, keepdims=True) + ssr) * inv_d + EPS) * scale
        kmla_ref[:, sl] = ((kh + krr) * inv).astype(BF16)
    vmla_ref[...] = jnp.dot(cb, wuv_ref[...], preferred_element_type=F32).astype(BF16)

    mq = _head_rms(y[:, 512:1024], gmq_ref[...], bd, 512)
    qmo_ref[...] = mq.astype(BF16)
    mk = _head_rms(y[:, 1024:1152], gmk_ref[...], bd, 128)
    mk_ref[...] = mk
    kmo_ref[...] = mk.astype(BF16)
    mv = y[:, 1152:1280]
    mv_ref[...] = mv
    vmo_ref[...] = mv.astype(BF16)
    tm = x.shape[0]
    for i in range(tm // MOBA_BLOCK):
        kmean_ref[0, i:i + 1, :] = jnp.mean(mk[MOBA_BLOCK * i:MOBA_BLOCK * (i + 1)], axis=0, keepdims=True)


def _even_proj(x, ga, wp, gql, gkv, wuq, gq, gkn, gkr, wuk, wuv, gmq, gmk, bd, cos_t, sin_t, *, tm, qk_dim, scale):
    n, d = x.shape
    nt = n // tm
    row = lambda c: pl.BlockSpec((tm, c), lambda i: (i, 0))
    full = lambda a: pl.BlockSpec(a.shape, lambda i: (0,) * a.ndim)
    consts = (ga, wp, gql, gkv, wuq, gq, gkn, gkr, wuk, wuv, gmq, gmk, bd)
    out_shape = (
        jax.ShapeDtypeStruct((n, 256), F32), jax.ShapeDtypeStruct((n, 128), F32),
        jax.ShapeDtypeStruct((n, 128), F32), jax.ShapeDtypeStruct((n, 128), F32),
        jax.ShapeDtypeStruct((n, 1024), BF16), jax.ShapeDtypeStruct((n, 1024), BF16),
        jax.ShapeDtypeStruct((n, 512), BF16), jax.ShapeDtypeStruct((n, 512), BF16),
        jax.ShapeDtypeStruct((n, 128), BF16), jax.ShapeDtypeStruct((n, 128), BF16),
        jax.ShapeDtypeStruct((nt, tm // MOBA_BLOCK, 128), F32),
    )
    out_specs = (row(256), row(128), row(128), row(128), row(1024), row(1024), row(512), row(512),
                 row(128), row(128), pl.BlockSpec((1, tm // MOBA_BLOCK, 128), lambda i: (i, 0, 0)))
    return pl.pallas_call(
        functools.partial(_even_proj_kernel, n_heads=8, qk_dim=qk_dim, scale=scale),
        grid=(nt,),
        in_specs=[row(d)] + [full(a) for a in consts] + [row(128), row(128)],
        out_specs=out_specs, out_shape=out_shape,
        compiler_params=_cp(("parallel",)), name="even_proj",
    )(x, *consts, cos_t, sin_t)


def _odd_proj_kernel(x_ref, ga_ref, wp_ref, bf_ref, gq_ref, gk_ref, bd_ref,
                     q_ref, k_ref, v_ref, lf_ref, kb_ref, vb_ref):
    x = x_ref[...]
    h = _rms(x, ga_ref[...]).astype(BF16)
    y = jnp.dot(h, wp_ref[...], preferred_element_type=F32)
    bd = bd_ref[...]
    q_ref[...] = _head_rms(y[:, 0:1024], gq_ref[...], bd, 1024).astype(BF16)
    k = _head_rms(y[:, 1024:1280], gk_ref[...], bd, 256)
    k_ref[...] = k
    kb_ref[...] = k.astype(BF16)
    v = y[:, 1280:1536]
    v_ref[...] = v
    vb_ref[...] = v.astype(BF16)
    z = y[:, 1536:1664] + bf_ref[...]
    lf_ref[...] = jnp.minimum(z, 0.0) - jnp.log1p(jnp.exp(-jnp.abs(z)))


def _odd_proj(x, ga, wp, bf, gq, gk, bd, *, tm):
    n, d = x.shape
    nt = n // tm
    row = lambda c: pl.BlockSpec((tm, c), lambda i: (i, 0))
    full = lambda a: pl.BlockSpec(a.shape, lambda i: (0,) * a.ndim)
    consts = (ga, wp, bf, gq, gk, bd)
    out_shape = (jax.ShapeDtypeStruct((n, 1024), BF16), jax.ShapeDtypeStruct((n, 256), F32),
                 jax.ShapeDtypeStruct((n, 256), F32), jax.ShapeDtypeStruct((n, 128), F32),
                 jax.ShapeDtypeStruct((n, 256), BF16), jax.ShapeDtypeStruct((n, 256), BF16))
    return pl.pallas_call(
        _odd_proj_kernel, grid=(nt,),
        in_specs=[row(d)] + [full(a) for a in consts],
        out_specs=(row(1024), row(256), row(256), row(128), row(256), row(256)), out_shape=out_shape,
        compiler_params=_cp(("parallel",)), name="odd_proj",
    )(x, *consts)


def _cumsum_kernel(lf_ref, tri_ref, eye_ref, cum_ref, cumt_ref, carry_ref, *, blk):
    carry_ref[...] = jnp.zeros_like(carry_ref)
    s = lf_ref.shape[1]
    tri = tri_ref[...]
    eye = eye_ref[...]
    for i in range(s // blk):
        lf = lf_ref[0, blk * i:blk * (i + 1), :]
        cum = _dot_exact_lhs(tri, lf, 3) + carry_ref[...]
        carry_ref[...] = cum[blk - 1:blk, :]
        cum_ref[0, blk * i:blk * (i + 1), :] = cum
        cumt_ref[0, :, blk * i:blk * (i + 1)] = _dot_nt_exact_lhs(eye, cum, 3)


def _prompt_cumsum(lf):
    b, s, w = lf.shape
    blk = 256
    tri = jnp.tril(jnp.ones((blk, blk), F32)).astype(BF16)
    eye = jnp.eye(w, dtype=BF16)
    return pl.pallas_call(
        functools.partial(_cumsum_kernel, blk=blk), grid=(b,),
        in_specs=[pl.BlockSpec((1, s, w), lambda i: (i, 0, 0)),
                  pl.BlockSpec((blk, blk), lambda i: (0, 0)), pl.BlockSpec((w, w), lambda i: (0, 0))],
        out_specs=(pl.BlockSpec((1, s, w), lambda i: (i, 0, 0)), pl.BlockSpec((1, w, s), lambda i: (i, 0, 0))),
        out_shape=(jax.ShapeDtypeStruct((b, s, w), F32), jax.ShapeDtypeStruct((b, w, s), F32)),
        scratch_shapes=[pltpu.VMEM((1, w), F32)],
        compiler_params=_cp(("parallel",)), name="prompt_cumsum",
    )(lf, tri, eye)


def _bias_tables_kernel(rb_ref, bown_ref, bprev_ref, b31_ref, bnew_ref,
                        town_ref, tprev_ref, t31_ref, tnew_ref, far_ref):
    h = pl.program_id(0)

    def lookup(bucket):
        out = jnp.full(bucket.shape, NEG, F32)
        for b in range(REL_BUCKETS):
            out = jnp.where(bucket == b, rb_ref[b, h] * LOG2E, out)
        return out

    town_ref[0] = lookup(bown_ref[...])
    tprev_ref[0] = lookup(bprev_ref[...])
    t31_ref[0] = lookup(b31_ref[...])
    tnew_ref[0] = lookup(bnew_ref[...])
    far_ref[0] = jnp.full(far_ref.shape[1:], rb_ref[REL_BUCKETS - 1, h] * LOG2E, F32)


def _bias_tables(rel_bias, bown, bprev, b31, bnew):
    n_heads = rel_bias.shape[1]
    full = lambda a: pl.BlockSpec(a.shape, lambda h: (0,) * a.ndim)
    per_head = lambda a: pl.BlockSpec((1,) + a.shape, lambda h: (h,) + (0,) * a.ndim)
    shp = lambda a: jax.ShapeDtypeStruct((n_heads,) + a.shape, F32)
    far = jax.ShapeDtypeStruct((n_heads, 8, LANES), F32)
    return pl.pallas_call(
        _bias_tables_kernel, grid=(n_heads,),
        in_specs=[pl.BlockSpec(memory_space=pltpu.SMEM), full(bown), full(bprev), full(b31), full(bnew)],
        out_specs=(per_head(bown), per_head(bprev), per_head(b31), per_head(bnew),
                   pl.BlockSpec((1, 8, LANES), lambda h: (h, 0, 0))),
        out_shape=(shp(bown), shp(bprev), shp(b31), shp(bnew), far),
        compiler_params=_cp(("parallel",)), name="bias_tables",
    )(rel_bias, bown, bprev, b31, bnew)


def _flash_kernel(*refs, mode, t):
    s_sc = refs[-2:]
    refs = refs[:-2]
    if mode == "mla":
        q_ref, k_ref, v_ref, o_ref, m_sc, acc_sc = refs
    elif mode == "fox":
        q_ref, k_ref, v_ref, cum_ref, cumt_ref, o_ref, m_sc, acc_sc = refs
    else:
        q_ref, k_ref, v_ref, kmean_ref, town_ref, tprev_ref, far_ref, o_ref, m_sc, acc_sc = refs
    hp = pl.program_id(1)
    qi = pl.program_id(2)
    lane = lax.broadcasted_iota(jnp.int32, (t, LANES), 1)
    upper = lane >= 64
    if mode == "mla":
        vhalf = (0, 1)
    else:
        khalf = (hp // 2) % 2
        vhalf = (khalf, khalf)

    q = []
    for a in range(2):
        if mode == "mla":
            q.append(q_ref[:, LANES * a:LANES * (a + 1)])
        else:
            qb = q_ref[...]
            src = jnp.where(khalf == a, qb, pltpu.roll(qb, 64, 1))
            q.append(jnp.where(upper.astype(jnp.int32) == khalf, src, jnp.zeros_like(src)))
    m_sc[...] = jnp.full(m_sc.shape, NEG, F32)
    acc_sc[...] = jnp.zeros(acc_sc.shape, F32)

    if mode == "fox":
        lane_c = lax.broadcasted_iota(jnp.int32, cum_ref.shape, 1)
        qc = [jnp.sum(jnp.where(lane_c == 2 * hp + a, cum_ref[...], 0.0), axis=-1, keepdims=True) * LOG2E
              for a in range(2)]
    if mode == "moba":
        n_sub = 16
        assert k_ref.shape[0] // MOBA_BLOCK <= n_sub
        km = kmean_ref[0, 0:n_sub, :]
        nb = t // MOBA_BLOCK
        blk = lax.broadcasted_iota(jnp.int32, (n_sub, t), 0)
        blk_of_lane = lax.broadcasted_iota(jnp.int32, (n_sub, t), 1) // MOBA_BLOCK
        blk_grp = lax.broadcasted_iota(jnp.int32, (LANES, nb * LANES), 1) // LANES
        blk_row = lax.broadcasted_iota(jnp.int32, (LANES, nb * LANES), 0)
        eye_sub = (lax.broadcasted_iota(jnp.int32, (n_sub, LANES), 0)
                   == lax.broadcasted_iota(jnp.int32, (n_sub, LANES), 1)).astype(BF16)
        own = nb * qi + blk_of_lane
        sel_t = []
        for a in range(2):
            gate_t = sum(_dot_nt(p, q[a]) for p in _split(km, 2))
            selected = _top_blocks(gate_t, blk, own, MOBA_TOPK, axis=0)
            sel_t.append(_dot_tn(jnp.where(selected, 0.0, NEG), eye_sub).astype(BF16))

    row_i = lax.broadcasted_iota(jnp.int32, (t, t), 0)
    col_i = lax.broadcasted_iota(jnp.int32, (t, t), 1)
    causal = row_i >= col_i

    def scores(j, slot):
        rows = pl.ds(pl.multiple_of(j * t, t), t)
        for a in range(2):
            k = k_ref[rows, LANES * a:LANES * (a + 1)] if mode == "mla" else k_ref[rows, :]
            s_sc[slot][a] = _dot_nt(q[a], k)

    def selection(j):
        if mode != "moba":
            return None
        onehot = (blk_row == nb * j + blk_grp).astype(BF16)
        return [jnp.dot(sel_t[a], onehot, preferred_element_type=F32) for a in range(2)]

    def attend(j, slot, kind, sels):
        rows = pl.ds(pl.multiple_of(j * t, t), t)
        v = v_ref[rows, :]
        ones = jnp.ones_like(v)
        for a in range(2):
            s = s_sc[slot][a]
            shift = None
            if mode == "fox":
                s = s - cumt_ref[0, pl.ds(2 * hp + a, 1), rows] * LOG2E
                shift = qc[a]
            if mode == "moba":
                sel = sels[a]
                far = far_ref[a, 0:1, 0:1]
                rep = MOBA_BLOCK // LANES
                if kind == "off":
                    sel = sel + far
                    s = s + jnp.concatenate([sel[:, LANES * (c // rep):LANES * (c // rep + 1)] for c in range(nb * rep)], axis=1)
                else:
                    base = nb if kind == "prev" else 0
                    bias_rows = []
                    for rb in range(nb):
                        quads = []
                        for cb in range(nb):
                            dist = base + rb - cb
                            sq = sel[MOBA_BLOCK * rb:MOBA_BLOCK * (rb + 1), LANES * cb:LANES * (cb + 1)]
                            sq = jnp.concatenate([sq] * rep, axis=1)
                            if dist == 0:
                                quads.append(town_ref[a])
                            elif dist == 1:
                                quads.append(tprev_ref[a] + sq)
                            elif dist >= 2:
                                quads.append(sq + far)
                            else:
                                quads.append(jnp.full((MOBA_BLOCK, MOBA_BLOCK), NEG, F32))
                        bias_rows.append(quads[0] if nb == 1 else jnp.concatenate(quads, axis=1))
                    s = s + (bias_rows[0] if nb == 1 else jnp.concatenate(bias_rows, axis=0))
            elif kind == "diag":
                s = jnp.where(causal, s, NEG)
            m_old = m_sc[a]
            row_max = jnp.max(s, axis=-1, keepdims=True)
            if shift is not None:
                m_new = jnp.maximum(m_old, row_max + shift)
                p = jnp.exp2(s - (m_new - shift))
            else:
                m_new = jnp.maximum(m_old, row_max)
                p = jnp.exp2(s - m_new)
            alpha = jnp.exp2(m_old - m_new)
            if mode == "mla":
                v_a = jnp.where(upper, v, ones) if a else jnp.where(upper, ones, v)
            else:
                v_a = jnp.where(upper.astype(jnp.int32) == khalf, v, ones)
            acc_sc[a] = alpha * acc_sc[a] + jnp.dot(p.astype(BF16), v_a, preferred_element_type=F32)
            m_sc[a] = m_new

    def step(j, slot, kind, ahead=None):
        sels = selection(j)
        if ahead is not None:
            scores(ahead, 1 - slot)
        attend(j, slot, kind, sels)

    def loop_body(i, carry):
        step(2 * i, 0, "off", ahead=2 * i + 1)
        step(2 * i + 1, 1, "off", ahead=2 * i + 2)
        return carry

    n_off = jnp.maximum(qi - 1, 0) if mode == "moba" else qi
    scores(0, 0)
    lax.fori_loop(0, n_off // 2, loop_body, 0)
    first = 2 * (n_off // 2)
    if mode == "moba":
        @pl.when(qi == 0)
        def _():
            step(0, 0, "diag")

        @pl.when(jnp.logical_and(qi >= 1, first == qi - 1))
        def _():
            step(qi - 1, 0, "prev", ahead=qi)
            step(qi, 1, "diag")

        @pl.when(jnp.logical_and(qi >= 1, first == qi - 2))
        def _():
            step(qi - 2, 0, "off", ahead=qi - 1)
            step(qi - 1, 1, "prev", ahead=qi)
            step(qi, 0, "diag")
    else:
        @pl.when(first == qi)
        def _():
            step(qi, 0, "diag")

        @pl.when(first == qi - 1)
        def _():
            step(qi - 1, 0, "off", ahead=qi)
            step(qi, 1, "diag")

    outs = []
    for a in range(2):
        acc = acc_sc[a]
        o = acc * (1.0 / pltpu.roll(acc, 64, 1))
        if mode != "mla":
            o = jnp.where(khalf == a, o, pltpu.roll(o, 64, 1))
        outs.append(o)
    o_ref[...] = jnp.where(upper, outs[1], outs[0]).astype(o_ref.dtype)


def _flash(mode, q, k, v, extra, *, b, s, t):
    nq = s // t
    if mode == "mla":
        n_pairs = v.shape[1] // LANES
        qspec = pl.BlockSpec((t, 2 * LANES), lambda bi, hp, qi: (bi * nq + qi, hp))
        kspec = pl.BlockSpec((s, 2 * LANES), lambda bi, hp, qi: (bi, hp))
        vspec = pl.BlockSpec((s, LANES), lambda bi, hp, qi: (bi, hp))
        extra_specs = []
    else:
        n_pairs = q.shape[1] // LANES
        group_pairs = n_pairs // (k.shape[1] // HEAD_DIM)
        qspec = pl.BlockSpec((t, LANES), lambda bi, hp, qi: (bi * nq + qi, hp))
        kspec = pl.BlockSpec((s, LANES), lambda bi, hp, qi: (bi, hp // (2 * group_pairs)))
        vspec = kspec
        if mode == "fox":
            cum, cumt = extra
            extra_specs = [pl.BlockSpec((None, t, cum.shape[2]), lambda bi, hp, qi: (bi, qi, 0)),
                           pl.BlockSpec((1, cumt.shape[1], s), lambda bi, hp, qi: (bi, 0, 0))]
        else:
            kmean, town, tprev, far = extra
            extra_specs = [pl.BlockSpec((1,) + kmean.shape[1:], lambda bi, hp, qi: (bi, 0, 0)),
                           pl.BlockSpec((2, MOBA_BLOCK, MOBA_BLOCK), lambda bi, hp, qi: (hp, 0, 0)),
                           pl.BlockSpec((2, MOBA_BLOCK, MOBA_BLOCK), lambda bi, hp, qi: (hp, 0, 0)),
                           pl.BlockSpec((2,) + far.shape[1:], lambda bi, hp, qi: (hp, 0, 0))]
    return pl.pallas_call(
        functools.partial(_flash_kernel, mode=mode, t=t),
        grid=(b, n_pairs, nq),
        in_specs=[qspec, kspec, vspec] + extra_specs,
        out_specs=pl.BlockSpec((t, LANES), lambda bi, hp, qi: (bi * nq + qi, hp)),
        out_shape=jax.ShapeDtypeStruct((b * s, n_pairs * LANES), BF16),
        scratch_shapes=[pltpu.VMEM((2, t, 1), F32), pltpu.VMEM((2, t, LANES), F32),
                        pltpu.VMEM((2, t, t), F32), pltpu.VMEM((2, t, t), F32)],
        compiler_params=_cp(("parallel", "parallel", "arbitrary")), name="flash_" + mode,
    )(q, k, v, *extra)


def _page_specs(layer, n_chunk_pages, shape, order, seq=0, n_seq=1):
    specs = []
    for i in range(n_chunk_pages):
        specs.append(pl.BlockSpec(
            (None, None) + shape,
            lambda b, j, pt, i=i: (layer, pt[n_seq * b + seq, order(j) * n_chunk_pages + i], 0, 0)))
    return specs


def _online_softmax(s, m_sc, l_sc, acc_sc, pv):
    m_old = m_sc[...]
    m_new = jnp.maximum(m_old, jnp.max(s, axis=-1, keepdims=True))
    alpha = jnp.exp2(m_old - m_new)
    p = jnp.exp2(s - m_new)
    l_sc[...] = alpha * l_sc[...] + jnp.sum(p, axis=-1, keepdims=True)
    acc_sc[...] = alpha * acc_sc[...] + pv(p)
    m_sc[...] = m_new


def _own_head_lanes(out, rows_per_kv):
    row = lax.broadcasted_iota(jnp.int32, out.shape, 0)
    lane = lax.broadcasted_iota(jnp.int32, out.shape, 1)
    y = jnp.where(lane // HEAD_DIM == row // rows_per_kv, out, 0.0)
    while y.shape[1] > LANES:
        half = y.shape[1] // 2
        y = y[:, :half] + y[:, half:]
    return y + pltpu.roll(y, 64, 1)


def _mla_sample_kernel(pt_ref, *refs, cp, n_chunks, qk_dim, nope, scale):
    c_refs = refs[:cp]
    kr_refs = refs[cp:2 * cp]
    (qbd_ref, qr_ref, cnew_ref, krnew_ref, cosp_ref, sinp_ref, cosn_ref, sinn_ref, wukt_ref, wuv_ref,
     gkr_ref, mask_ref, o_ref, m_sc, l_sc, acc_sc, qabs_sc) = refs[2 * cp:]
    j = pl.program_id(1)
    tn = cnew_ref.shape[1]
    n_heads = wuv_ref.shape[0]

    @pl.when(j == 0)
    def _():
        m_sc[...] = jnp.full(m_sc.shape, NEG, F32)
        l_sc[...] = jnp.zeros(l_sc.shape, F32)
        acc_sc[...] = jnp.zeros(acc_sc.shape, F32)
        qabs_sc[...] = _dot(qbd_ref[0], wukt_ref[...]).astype(BF16)

    def scores(c, krt, cos_t, sin_t):
        cb = c.astype(BF16)
        knt = _dot_nt(wukt_ref[...], cb)
        sq = knt * knt
        ssr = jnp.sum(krt * krt, axis=0, keepdims=True)
        inv = []
        for hh in range(n_heads):
            ss = jnp.sum(sq[nope * hh:nope * (hh + 1)], axis=0, keepdims=True) + ssr
            inv.append(lax.rsqrt(ss * (1.0 / qk_dim) + EPS) * scale)
        inv = _repeat_rows(jnp.concatenate(inv, axis=0), tn)
        krg = krt * gkr_ref[...]
        half = krt.shape[0] // 2
        krr = krg * cos_t + jnp.concatenate([krg[half:], krg[:half]], axis=0) * sin_t
        return cb, (_dot_nt(qabs_sc[...], cb) + _dot(qr_ref[0], krr)) * inv

    n_sub = 4 if cp % 4 == 0 else 1
    pps = cp // n_sub
    cbs, ss = [], []
    for i in range(n_sub):
        c = jnp.concatenate([r[...] for r in c_refs[pps * i:pps * (i + 1)]], axis=0)
        krt = jnp.concatenate([r[...] for r in kr_refs[pps * i:pps * (i + 1)]], axis=1)
        cols = slice(pps * 128 * i, pps * 128 * (i + 1))
        cb, s = scores(c, krt, cosp_ref[:, cols], sinp_ref[:, cols])
        cbs.append(cb)
        ss.append(s)
    cb_all = jnp.concatenate(cbs, axis=0)
    _online_softmax(jnp.concatenate(ss, axis=1), m_sc, l_sc, acc_sc, lambda p: _dot(p, cb_all))

    @pl.when(j == n_chunks - 1)
    def _():
        cb, s = scores(cnew_ref[0], krnew_ref[0], cosn_ref[...], sinn_ref[...])
        _online_softmax(jnp.where(mask_ref[...] > 0.5, s, NEG), m_sc, l_sc, acc_sc, lambda p: _dot(p, cb))
        pc = acc_sc[...] * (1.0 / l_sc[...])
        for hh in range(n_heads):
            o_ref[0, tn * hh:tn * (hh + 1), :] = _dot(pc[tn * hh:tn * (hh + 1), :], wuv_ref[hh])


def _mla_sample(page_table, cache_c, cache_krt, layer, qbd, qr, c_new, krt_new, cos_p, sin_p, cos_n, sin_n,
                wukt, wuv, gkr, mask, *, qk_dim, nope, scale):
    db, n_pages = page_table.shape
    cp = min(CHUNK_PAGES, n_pages)
    n_chunks = n_pages // cp
    tn = c_new.shape[1]
    n_heads = wuv.shape[0]
    rows = n_heads * tn
    fwd = lambda j: j
    per_seq = lambda a: pl.BlockSpec((1,) + a.shape[1:], lambda b, j, pt: (b,) + (0,) * (a.ndim - 1))
    full = lambda a: pl.BlockSpec(a.shape, lambda b, j, pt: (0,) * a.ndim)
    chunk_cols = lambda a: pl.BlockSpec((a.shape[0], cp * 128), lambda b, j, pt: (0, j))
    in_specs = (_page_specs(layer, cp, cache_c.shape[2:], fwd) + _page_specs(layer, cp, cache_krt.shape[2:], fwd)
                + [per_seq(qbd), per_seq(qr), per_seq(c_new), per_seq(krt_new), chunk_cols(cos_p), chunk_cols(sin_p),
                   full(cos_n), full(sin_n), full(wukt), full(wuv), full(gkr), full(mask)])
    grid_spec = pltpu.PrefetchScalarGridSpec(
        num_scalar_prefetch=1, grid=(db, n_chunks), in_specs=in_specs,
        out_specs=pl.BlockSpec((1, rows, wuv.shape[2]), lambda b, j, pt: (b, 0, 0)),
        scratch_shapes=[pltpu.VMEM((rows, 1), F32), pltpu.VMEM((rows, 1), F32), pltpu.VMEM((rows, cache_c.shape[-1]), F32),
                        pltpu.VMEM((rows, cache_c.shape[-1]), BF16)])
    return pl.pallas_call(
        functools.partial(_mla_sample_kernel, cp=cp, n_chunks=n_chunks, qk_dim=qk_dim, nope=nope, scale=scale),
        grid_spec=grid_spec,
        out_shape=jax.ShapeDtypeStruct((db, rows, wuv.shape[2]), F32),
        compiler_params=_cp(("parallel", "arbitrary")), name="mla_sample",
    )(page_table, *([cache_c] * cp), *([cache_krt] * cp), qbd, qr, c_new, krt_new, cos_p, sin_p, cos_n, sin_n,
      wukt, wuv, gkr, mask)


def _moba_sample_kernel(pt_ref, *refs, cp, n_chunks, n_blocks, rows_per_kv):
    k_refs = refs[:cp]
    v_refs = refs[cp:2 * cp]
    (q_ref, knew_ref, vnew_ref, t31_ref, tnew_ref, far_ref, ind_ref, expand_ref, o_ref,
     kt_all, vt_all, g_all) = refs[2 * cp:]
    j = pl.program_id(1)
    q = q_ref[0]
    bpc = cp * 128 // MOBA_BLOCK
    tk = cp * 128

    kt = jnp.concatenate([r[...] for r in k_refs], axis=1)
    vt = jnp.concatenate([r[...] for r in v_refs], axis=1)
    cols = pl.ds(pl.multiple_of(j * tk, tk), tk)
    kt_all[:, cols] = kt.astype(BF16)
    vt_all[:, cols] = vt.astype(BF16)
    kmean = _dot_exact_rhs(kt, ind_ref[...], 2) * (1.0 / MOBA_BLOCK)
    gate = pltpu.roll(_dot_exact_lhs(q, kmean, 2), j * bpc, 1)

    @pl.when(j == 0)
    def _():
        g_all[...] = gate

    @pl.when(j > 0)
    def _():
        g_all[...] += gate

    @pl.when(j == n_chunks - 1)
    def _():
        lane = lax.broadcasted_iota(jnp.int32, g_all.shape, 1)
        picked = _top_blocks(g_all[...], lane, n_blocks, min(MOBA_TOPK, n_blocks + 1), axis=1)
        selb = jnp.where(picked, 0.0, NEG).astype(BF16)
        s = _dot(q, kt_all[...]) + jnp.dot(selb, expand_ref[...], preferred_element_type=F32)
        last = s.shape[1] - MOBA_BLOCK
        s = jnp.concatenate([s[:, :last] + far_ref[...], s[:, last:] + t31_ref[...]], axis=1)
        s_new = _dot_nt(q, knew_ref[0]) + tnew_ref[...]
        m = jnp.maximum(jnp.max(s, axis=-1, keepdims=True), jnp.max(s_new, axis=-1, keepdims=True))
        p = jnp.exp2(s - m)
        p_new = jnp.exp2(s_new - m)
        l = jnp.sum(p, axis=-1, keepdims=True) + jnp.sum(p_new, axis=-1, keepdims=True)
        out = (_dot_nt(p, vt_all[...]) + _dot(p_new, vnew_ref[0])) * (1.0 / l)
        o_ref[0] = _own_head_lanes(out, rows_per_kv)


def _moba_sample(page_table, cache_kt, cache_vt, layer, q, k_new, v_new, t31, tnew, far, *, rows_per_kv):
    db, n_pages = page_table.shape
    cp = min(CHUNK_PAGES, n_pages)
    n_chunks = n_pages // cp
    past = n_pages * 128
    n_blocks = past // MOBA_BLOCK
    assert n_blocks <= LANES
    rows = q.shape[1]
    key_blk = jnp.arange(past) // MOBA_BLOCK
    ind = (key_blk[:cp * 128, None] == jnp.arange(LANES)[None, :]).astype(BF16)
    expand = (jnp.arange(LANES)[:, None] == key_blk[None, :]).astype(BF16)
    fwd = lambda j: j
    per_seq = lambda a: pl.BlockSpec((1,) + a.shape[1:], lambda b, j, pt: (b,) + (0,) * (a.ndim - 1))
    full = lambda a: pl.BlockSpec(a.shape, lambda b, j, pt: (0,) * a.ndim)
    in_specs = (_page_specs(layer, cp, cache_kt.shape[2:], fwd) + _page_specs(layer, cp, cache_vt.shape[2:], fwd)
                + [per_seq(q), per_seq(k_new), per_seq(v_new), full(t31), full(tnew), full(far), full(ind), full(expand)])
    grid_spec = pltpu.PrefetchScalarGridSpec(
        num_scalar_prefetch=1, grid=(db, n_chunks), in_specs=in_specs,
        out_specs=pl.BlockSpec((1, rows, LANES), lambda b, j, pt: (b, 0, 0)),
        scratch_shapes=[pltpu.VMEM((cache_kt.shape[2], past), BF16), pltpu.VMEM((cache_vt.shape[2], past), BF16),
                        pltpu.VMEM((rows, LANES), F32)])
    return pl.pallas_call(
        functools.partial(_moba_sample_kernel, cp=cp, n_chunks=n_chunks, n_blocks=n_blocks, rows_per_kv=rows_per_kv),
        grid_spec=grid_spec, out_shape=jax.ShapeDtypeStruct((db, rows, LANES), F32),
        compiler_params=_cp(("parallel", "arbitrary")), name="moba_sample",
    )(page_table, *([cache_kt] * cp), *([cache_vt] * cp), q, k_new, v_new, t31, tnew, far, ind, expand)


def _fox_sample_kernel(pt_ref, *refs, cp, n_chunks, rows_per_kv, ns):
    shared = refs[3 * ns * cp:]
    per_seq, consts, o_ref, scratch = shared[:4], shared[4:8], shared[8], shared[9:]
    j = pl.program_id(1)

    def seq_args(u):
        pages = [refs[(g * ns + u) * cp:(g * ns + u + 1) * cp] for g in range(3)]
        return (*pages, *[r.at[u] for r in per_seq], *consts, o_ref.at[u], *[r.at[u] for r in scratch])

    @pl.when(j == 0)
    def _():
        for u in range(ns):
            _fox_one_seq("first", *seq_args(u), rows_per_kv=rows_per_kv)

    for u in range(ns):
        _fox_one_seq("chunk", *seq_args(u), rows_per_kv=rows_per_kv)

    @pl.when(j == n_chunks - 1)
    def _():
        for u in range(ns):
            _fox_one_seq("last", *seq_args(u), rows_per_kv=rows_per_kv)


def _fox_one_seq(phase, k_refs, v_refs, lf_refs, q_ref, knew_ref, vnew_ref, lftnew_ref, rep_ref, tri_ref, tmask_ref,
                 cmask_ref, o_ref, m_sc, l_sc, acc_sc, carry_sc, ncol_sc, *, rows_per_kv):
    q = q_ref[...]
    n_h = lftnew_ref.shape[0]
    reps = q.shape[0] // n_h

    if phase == "last":
        o_ref[...] = _own_head_lanes(acc_sc[...] * (1.0 / l_sc[...]), rows_per_kv)
        return
    if phase == "first":
        m_sc[...] = jnp.full(m_sc.shape, NEG, F32)
        l_sc[...] = jnp.zeros(l_sc.shape, F32)
        acc_sc[...] = jnp.zeros(acc_sc.shape, F32)
        carry_sc[...] = jnp.zeros(carry_sc.shape, F32)
        n_t = _dot_exact_rhs(lftnew_ref[...] * LOG2E, tri_ref[...], 3)
        nmat = _dot_exact_lhs(rep_ref[...], n_t, 3)
        ncol = jnp.sum(nmat * tmask_ref[...], axis=-1, keepdims=True)
        ncol_sc[...] = ncol
        s = _dot_nt(q, knew_ref[...]) + (ncol - nmat)
        _online_softmax(jnp.where(cmask_ref[...] > 0.5, s, NEG), m_sc, l_sc, acc_sc, lambda p: _dot(p, vnew_ref[...]))
        return

    kt = jnp.concatenate([r[...] for r in k_refs], axis=1)
    vt = jnp.concatenate([r[...] for r in v_refs], axis=1)
    lft = jnp.concatenate([r[...] for r in lf_refs], axis=1) * LOG2E
    tk = lft.shape[1]
    lane = lax.broadcasted_iota(jnp.int32, lft.shape, 1)
    x = lft
    sh = 1
    while sh < tk:
        if sh % LANES == 0:
            shifted = jnp.concatenate([x[:, sh:], jnp.zeros((n_h, sh), F32)], axis=1)
        else:
            shifted = jnp.where(lane < tk - sh, pltpu.roll(x, tk - sh, 1), 0.0)
        x = x + shifted
        sh *= 2
    carry = carry_sc[...]
    later = x - lft + carry
    carry_sc[...] = carry + x[:, 0:1]
    s = _dot(q, kt) + _repeat_rows(later, reps) + ncol_sc[...]
    _online_softmax(s, m_sc, l_sc, acc_sc, lambda p: _dot_nt(p, vt))


def _fox_sample(page_table, cache_kt, cache_vt, cache_lft, layer, q, k_new, v_new, lft_new, rep, tri, tmask, cmask,
                *, rows_per_kv):
    db, n_pages = page_table.shape
    cp = min(CHUNK_PAGES, n_pages)
    n_chunks = n_pages // cp
    rows = q.shape[1]
    ns = SEQS_PER_STEP if db % SEQS_PER_STEP == 0 else 1
    rev = lambda j: n_chunks - 1 - j
    per_seq = lambda a: pl.BlockSpec((ns,) + a.shape[1:], lambda b, j, pt: (b,) + (0,) * (a.ndim - 1))
    full = lambda a: pl.BlockSpec(a.shape, lambda b, j, pt: (0,) * a.ndim)
    in_specs = []
    for cache in (cache_kt, cache_vt, cache_lft):
        for u in range(ns):
            in_specs += _page_specs(layer, cp, cache.shape[2:], rev, u, ns)
    in_specs += [per_seq(q), per_seq(k_new), per_seq(v_new), per_seq(lft_new), full(rep), full(tri), full(tmask), full(cmask)]
    grid_spec = pltpu.PrefetchScalarGridSpec(
        num_scalar_prefetch=1, grid=(db // ns, n_chunks), in_specs=in_specs,
        out_specs=pl.BlockSpec((ns, rows, LANES), lambda b, j, pt: (b, 0, 0)),
        scratch_shapes=[pltpu.VMEM((ns, rows, 1), F32), pltpu.VMEM((ns, rows, 1), F32), pltpu.VMEM((ns, rows, q.shape[2]), F32),
                        pltpu.VMEM((ns, lft_new.shape[1], 1), F32), pltpu.VMEM((ns, rows, 1), F32)])
    return pl.pallas_call(
        functools.partial(_fox_sample_kernel, cp=cp, n_chunks=n_chunks, rows_per_kv=rows_per_kv, ns=ns),
        grid_spec=grid_spec, out_shape=jax.ShapeDtypeStruct((db, rows, LANES), F32),
        compiler_params=_cp(("parallel", "arbitrary")), name="fox_sample",
    )(page_table, *([cache_kt] * (cp * ns)), *([cache_vt] * (cp * ns)), *([cache_lft] * (cp * ns)), q, k_new, v_new, lft_new,
      rep, tri, tmask, cmask)


def _mix_mlp_kernel(x_ref, o_ref, wo_ref, g_ref, up_ref, down_ref, y_ref, x1_sc, h_sc, acc_sc):
    f = pl.program_id(1)

    @pl.when(f == 0)
    def _():
        x1 = x_ref[...] + jnp.dot(o_ref[...], wo_ref[...], preferred_element_type=F32)
        x1_sc[...] = x1
        h_sc[...] = _rms(x1, g_ref[...]).astype(BF16)
        acc_sc[...] = jnp.zeros(acc_sc.shape, F32)

    u = jnp.maximum(jnp.dot(h_sc[...], up_ref[...], preferred_element_type=F32), 0.0)
    acc_sc[...] += jnp.dot((u * u).astype(BF16), down_ref[...], preferred_element_type=F32)

    @pl.when(f == pl.num_programs(1) - 1)
    def _():
        y_ref[...] = x1_sc[...] + acc_sc[...]


def _mix_mlp(x, o, wo, g, up, down, *, tm):
    n, d = x.shape
    dff = up.shape[1]
    tf = min(FF_TILE, dff)
    return pl.pallas_call(
        _mix_mlp_kernel, grid=(n // tm, dff // tf),
        in_specs=[pl.BlockSpec((tm, d), lambda i, f: (i, 0)), pl.BlockSpec((tm, o.shape[1]), lambda i, f: (i, 0)),
                  pl.BlockSpec(wo.shape, lambda i, f: (0, 0)), pl.BlockSpec(g.shape, lambda i, f: (0, 0)),
                  pl.BlockSpec((d, tf), lambda i, f: (0, f)), pl.BlockSpec((tf, d), lambda i, f: (f, 0))],
        out_specs=pl.BlockSpec((tm, d), lambda i, f: (i, 0)),
        out_shape=jax.ShapeDtypeStruct((n, d), F32),
        scratch_shapes=[pltpu.VMEM((tm, d), F32), pltpu.VMEM((tm, d), BF16), pltpu.VMEM((tm, d), F32)],
        compiler_params=_cp(("parallel", "arbitrary")), name="mix_mlp",
    )(x, o, wo, g, up, down)


def _t5_bucket(rel):
    n = jnp.maximum(rel, 0)
    exact = REL_BUCKETS // 2
    scaled = jnp.log(jnp.maximum(n, 1).astype(F32) / exact) / math.log(REL_MAX_DIST / exact)
    large = exact + (scaled * (REL_BUCKETS - exact)).astype(jnp.int32)
    return jnp.where(n < exact, n, jnp.minimum(large, REL_BUCKETS - 1))


def _rope_tables(pos, half):
    inv = ROPE_THETA ** (-jnp.arange(half, dtype=F32) / half)
    ang = pos.astype(F32)[:, None] * inv[None, :]
    return jnp.cos(ang), jnp.sin(ang)


def _pick_tile(n, cap):
    t = cap
    while n % t:
        t //= 2
    return t


def _pages_on_lanes(cache):
    nd = cache.ndim
    t = jnp.transpose(cache, (0, 1) + tuple(range(3, nd)) + (2,))
    return t.reshape(t.shape[:2] + (-1, t.shape[-1]))


def kernel(x_prompt, x_sample, cache_mla_latent, cache_mla_krope, cache_moba_k, cache_moba_v, cache_fox_k, cache_fox_v, cache_fox_logf, page_table, rel_bias, g_attn, g_mlp, w_mlp_up, w_mlp_down, w_in_even, g_q_lat, g_kv_lat, w_uq, w_uk, w_uv, g_mla_q, g_mla_k, g_moba_q, g_moba_k, w_o_even, w_in_odd, b_forget, g_fox_q, g_fox_k, w_o_odd):
    B, S, D = x_prompt.shape
    DB, TN, _ = x_sample.shape
    depth = g_attn.shape[0]
    n_pages = page_table.shape[1]
    page = cache_mla_latent.shape[2]
    past = n_pages * page
    n_p, n_s = B * S, DB * TN
    n = n_p + n_s
    q_lora, kv_lora = g_q_lat.shape[1], g_kv_lat.shape[1]
    mla_heads, nope, mla_v = w_uk.shape[2], w_uk.shape[3], w_uv.shape[3]
    qk_dim = w_uq.shape[3]
    rope_d = qk_dim - nope
    half = rope_d // 2
    moba_kv = cache_moba_k.shape[3]
    moba_heads = rel_bias.shape[1]
    moba_g = moba_heads // moba_kv
    fox_heads = b_forget.shape[1]
    fox_kv = cache_fox_k.shape[3]
    fox_g = fox_heads // fox_kv
    mla_scale = float(qk_dim) ** -0.5
    att_scale = HEAD_DIM ** -0.5
    assert (q_lora, kv_lora, mla_heads, nope, rope_d, mla_v) == (256, 256, 8, 64, 32, 64)
    assert (moba_heads, moba_kv, fox_heads, fox_kv, page, TN) == (8, 2, 16, 4, 128, 8)
    assert past % MOBA_BLOCK == 0 and S % 512 == 0 and n_p % 256 == 0 and n_s % 256 == 0
    assert MOBA_BLOCK + 1 >= REL_MAX_DIST
    tm = _pick_tile(math.gcd(n_p, n_s), TOKEN_TILE)

    pos = jnp.concatenate([jnp.tile(jnp.arange(S, dtype=jnp.int32), B),
                           jnp.tile(past + jnp.arange(TN, dtype=jnp.int32), DB)])
    cos, sin = _rope_tables(pos, half)
    ones64 = jnp.ones((n, 64), F32)
    zeros32 = jnp.zeros((n, 32), F32)
    cos_t = jnp.concatenate([ones64, cos, cos, zeros32], axis=1)
    sin_t = jnp.concatenate([jnp.zeros((n, 64), F32), -sin, sin, zeros32], axis=1)
    kcos, ksin = _rope_tables(jnp.arange(past + TN, dtype=jnp.int32), half)
    kcos_t = jnp.concatenate([kcos, kcos], axis=1).T
    ksin_t = jnp.concatenate([-ksin, ksin], axis=1).T

    bd = (jnp.arange(512)[:, None] // HEAD_DIM == jnp.arange(512)[None, :] // HEAD_DIM).astype(BF16)
    t_new = jnp.arange(TN)
    t_mla = jnp.arange(mla_heads * TN) % TN
    mla_mask = (t_new[None, :] <= t_mla[:, None]).astype(F32)
    r_fox = jnp.arange(fox_heads * TN)
    rep_fox = (r_fox[:, None] // TN == jnp.arange(fox_heads)[None, :]).astype(BF16)
    tri_new = (t_new[:, None] <= t_new[None, :]).astype(BF16)
    tmask_fox = (t_new[None, :] == (r_fox % TN)[:, None]).astype(F32)
    cmask_fox = (t_new[None, :] <= (r_fox % TN)[:, None]).astype(F32)

    ii = jnp.arange(MOBA_BLOCK)
    rel_own = ii[:, None] - ii[None, :]
    b_own = jnp.where(rel_own >= 0, _t5_bucket(rel_own), -1)
    b_prev = _t5_bucket(rel_own + MOBA_BLOCK)
    b_31 = _t5_bucket(MOBA_BLOCK + t_new[:, None] - ii[None, :])
    rel_new = t_new[:, None] - t_new[None, :]
    b_new = jnp.where(rel_new >= 0, _t5_bucket(rel_new), -1)
    b_new = jnp.pad(b_new, ((0, 0), (0, LANES - TN)), constant_values=-1)
    town, tprev, t31, tnew, far = _bias_tables(rel_bias.astype(F32), b_own.astype(jnp.int32), b_prev.astype(jnp.int32),
                                               b_31.astype(jnp.int32), b_new.astype(jnp.int32))
    t31_s = t31.reshape(moba_heads * TN, MOBA_BLOCK)
    tnew_s = tnew.reshape(moba_heads * TN, LANES)[:, :TN]
    far_s = far.reshape(moba_heads * 8, LANES)[:, :1]

    krope_t = _pages_on_lanes(cache_mla_krope)
    moba_kt, moba_vt = _pages_on_lanes(cache_moba_k), _pages_on_lanes(cache_moba_v)
    fox_kt, fox_vt = _pages_on_lanes(cache_fox_k), _pages_on_lanes(cache_fox_v)
    fox_lft = _pages_on_lanes(cache_fox_logf)

    pt = page_table.astype(jnp.int32)
    x = jnp.concatenate([x_prompt.reshape(n_p, D), x_sample.reshape(n_s, D)], axis=0)
    outs = {k: [] for k in ("lat", "kr", "mk", "mv", "fk", "fv", "lf")}

    for layer in range(depth):
        ga = g_attn[layer][None, :]
        if layer % 2 == 0:
            e = layer // 2
            w = w_in_even[e]
            wp = jnp.concatenate([w[:, 0:512], w[:, 544:1312], jnp.zeros((D, nope), F32), w[:, 512:544],
                                  jnp.zeros((D, LANES - qk_dim), F32)], axis=1).astype(BF16)
            wuq = jnp.pad(w_uq[e], ((0, 0), (0, 0), (0, LANES - qk_dim))).reshape(q_lora, mla_heads * LANES).astype(BF16)
            wuk_pad = jnp.pad(w_uk[e], ((0, 0), (0, 0), (0, LANES - nope))).reshape(kv_lora, mla_heads * LANES).astype(BF16)
            wukt = w_uk[e].transpose(1, 2, 0).reshape(mla_heads * nope, kv_lora).astype(BF16)
            wuv = w_uv[e].reshape(kv_lora, mla_heads * mla_v).astype(BF16)
            wuv_h = w_uv[e].transpose(1, 0, 2).astype(BF16)
            gq = jnp.pad(g_mla_q[e], (0, LANES - qk_dim))[None, :]
            gkn = jnp.concatenate([g_mla_k[e][:nope], jnp.ones((LANES - nope,), F32)])[None, :]
            gkr = jnp.concatenate([jnp.zeros((nope,), F32), g_mla_k[e][nope:], jnp.zeros((LANES - qk_dim,), F32)])[None, :]
            gmq = (jnp.tile(g_moba_q[e], moba_heads) * (att_scale * LOG2E))[None, :]
            gmk = jnp.tile(g_moba_k[e], moba_kv)[None, :]
            (c, krb, mk, mv, qmla, kmla, vmla, qmo, kmo, vmo, kmean) = _even_proj(
                x, ga, wp, g_q_lat[e][None, :], g_kv_lat[e][None, :], wuq, gq, gkn, gkr, wuk_pad, wuv, gmq, gmk, bd,
                cos_t, sin_t, tm=tm, qk_dim=qk_dim, scale=mla_scale * LOG2E)
            kr = krb[:, 64:64 + rope_d]
            o1_p = _flash("mla", qmla, kmla, vmla, (), b=B, s=S, t=512)
            nblk = S // MOBA_BLOCK
            km_p = kmean.reshape(n // MOBA_BLOCK, 128)[:B * nblk].reshape(B, nblk, 128)
            km_p = jnp.pad(km_p, ((0, 0), (0, LANES - nblk), (0, 0)))
            o2_p = _flash("moba", qmo, kmo, vmo, (km_p, town, tprev, far), b=B, s=S, t=512)
            qs = qmla[n_p:].reshape(DB, TN, mla_heads, LANES).transpose(0, 2, 1, 3)
            qbd = jnp.einsum('bhtd,hj->bhtjd', qs[..., :nope], jnp.eye(mla_heads, dtype=BF16))
            qbd = qbd.reshape(DB, mla_heads * TN, mla_heads * nope)
            qr = qs[..., nope:qk_dim].reshape(DB, mla_heads * TN, rope_d)
            o1_s = _mla_sample(pt, cache_mla_latent, krope_t, e, qbd, qr,
                               c[n_p:].reshape(DB, TN, kv_lora), kr[n_p:].reshape(DB, TN, rope_d).transpose(0, 2, 1),
                               kcos_t[:, :past], ksin_t[:, :past], kcos_t[:, past:], ksin_t[:, past:],
                               wukt, wuv_h, g_mla_k[e][nope:][:, None], mla_mask,
                               qk_dim=qk_dim, nope=nope, scale=mla_scale * LOG2E)
            o1_s = o1_s.reshape(DB, mla_heads, TN, mla_v).transpose(0, 2, 1, 3).reshape(n_s, mla_heads * mla_v)
            q2 = qmo[n_p:].reshape(DB, TN, moba_kv, moba_g, HEAD_DIM).transpose(0, 2, 3, 1, 4)
            q2 = jnp.einsum('bkgtd,kj->bkgtjd', q2, jnp.eye(moba_kv, dtype=BF16)).reshape(DB, moba_heads * TN, moba_kv * HEAD_DIM)
            o2_s = _moba_sample(pt, moba_kt, moba_vt, e, q2, mk[n_p:].reshape(DB, TN, -1), mv[n_p:].reshape(DB, TN, -1),
                                t31_s, tnew_s, far_s, rows_per_kv=moba_g * TN)
            o2_s = o2_s[:, :, :HEAD_DIM].reshape(DB, moba_heads, TN, HEAD_DIM).transpose(0, 2, 1, 3)
            o2_s = o2_s.reshape(n_s, moba_heads * HEAD_DIM)
            o = jnp.concatenate([jnp.concatenate([o1_p, o2_p], axis=1),
                                 jnp.concatenate([o1_s.astype(BF16), o2_s.astype(BF16)], axis=1)], axis=0)
            wo = w_o_even[e].astype(BF16)
            outs["lat"].append(c); outs["kr"].append(kr); outs["mk"].append(mk); outs["mv"].append(mv)
        else:
            jl = layer // 2
            w = w_in_odd[jl]
            wp = jnp.concatenate([w, jnp.zeros((D, LANES - fox_heads), F32)], axis=1).astype(BF16)
            bf = jnp.pad(b_forget[jl], (0, LANES - fox_heads))[None, :]
            gq = (jnp.tile(g_fox_q[jl], fox_heads) * (att_scale * LOG2E))[None, :]
            gk = jnp.tile(g_fox_k[jl], fox_kv)[None, :]
            q, k, v, lf, kb, vb = _odd_proj(x, ga, wp, bf, gq, gk, bd, tm=tm)
            cum, cumt = _prompt_cumsum(lf[:n_p].reshape(B, S, LANES))
            o_p = _flash("fox", q, kb, vb, (cum, cumt), b=B, s=S, t=512)
            qf = q[n_p:].reshape(DB, TN, fox_kv, fox_g, HEAD_DIM).transpose(0, 2, 3, 1, 4)
            qf = jnp.einsum('bkgtd,kj->bkgtjd', qf, jnp.eye(fox_kv, dtype=BF16)).reshape(DB, fox_heads * TN, fox_kv * HEAD_DIM)
            lfh = lf[:, :fox_heads]
            o_s = _fox_sample(pt, fox_kt, fox_vt, fox_lft, jl, qf,
                              k[n_p:].reshape(DB, TN, -1), v[n_p:].reshape(DB, TN, -1),
                              lfh[n_p:].reshape(DB, TN, fox_heads).transpose(0, 2, 1),
                              rep_fox, tri_new, tmask_fox, cmask_fox, rows_per_kv=fox_g * TN)
            o_s = o_s[:, :, :HEAD_DIM].reshape(DB, fox_heads, TN, HEAD_DIM).transpose(0, 2, 1, 3).reshape(n_s, fox_heads * HEAD_DIM)
            o = jnp.concatenate([o_p, o_s.astype(BF16)], axis=0)
            wo = w_o_odd[jl].astype(BF16)
            outs["fk"].append(k); outs["fv"].append(v); outs["lf"].append(lfh)
        x = _mix_mlp(x, o, wo, g_mlp[layer][None, :], w_mlp_up[layer].astype(BF16), w_mlp_down[layer].astype(BF16), tm=tm)

    def split(name, tail):
        a = jnp.stack(outs[name])
        return a[:, :n_p].reshape((a.shape[0], B, S) + tail), a[:, n_p:].reshape((a.shape[0], DB, TN) + tail)

    lat_p, lat_s = split("lat", (kv_lora,))
    kr_p, kr_s = split("kr", (rope_d,))
    mk_p, mk_s = split("mk", (moba_kv, HEAD_DIM))
    mv_p, mv_s = split("mv", (moba_kv, HEAD_DIM))
    fk_p, fk_s = split("fk", (fox_kv, HEAD_DIM))
    fv_p, fv_s = split("fv", (fox_kv, HEAD_DIM))
    lf_p, lf_s = split("lf", (fox_heads,))
    return (x[:n_p].reshape(B, S, D), x[n_p:].reshape(DB, TN, D), lat_p, lat_s, kr_p, kr_s, mk_p, mk_s, mv_p, mv_s,
            fk_p, fk_s, fv_p, fv_s, lf_p, lf_s)
```

```python
import functools
import math

import jax
import jax.numpy as jnp
from jax import lax
from jax.experimental import pallas as pl
from jax.experimental.pallas import tpu as pltpu

F32 = jnp.float32
BF16 = jnp.bfloat16
NEG = -1e30
EPS = 1e-6
LOG2E = math.log2(math.e)
ROPE_THETA = 10000.0
HEAD_DIM = 64
MOBA_BLOCK = 256
MOBA_TOPK = 3
REL_BUCKETS = 32
REL_MAX_DIST = 128
LANES = 128
TOKEN_TILE = 512
FF_TILE = 2048
CHUNK_PAGES = 16
SEQS_PER_STEP = 2
VMEM_LIMIT = 56 * 1024 * 1024


def _cp(sem, vmem=VMEM_LIMIT):
    return pltpu.CompilerParams(dimension_semantics=sem, vmem_limit_bytes=vmem)


def _dot(a, b):
    return jnp.dot(a.astype(BF16), b.astype(BF16), preferred_element_type=F32)


def _dot_nt(a, b):
    return lax.dot_general(a.astype(BF16), b.astype(BF16), (((1,), (1,)), ((), ())),
                           preferred_element_type=F32)


def _dot_tn(a, b):
    return lax.dot_general(a.astype(BF16), b.astype(BF16), (((0,), (0,)), ((), ())),
                           preferred_element_type=F32)


def _split(a, n):
    parts = []
    r = a.astype(F32)
    for _ in range(n):
        p = r.astype(BF16)
        parts.append(p)
        r = r - p.astype(F32)
    return parts


def _dot_exact_rhs(a, b, n=3):
    return sum(jnp.dot(p, b, preferred_element_type=F32) for p in _split(a, n))


def _dot_exact_lhs(a, b, n=3):
    return sum(jnp.dot(a, p, preferred_element_type=F32) for p in _split(b, n))


def _dot_nt_exact_lhs(a, b, n=3):
    return sum(lax.dot_general(a, p, (((1,), (1,)), ((), ())), preferred_element_type=F32)
               for p in _split(b, n))


def _rms(v, g):
    return v * lax.rsqrt(jnp.mean(v * v, axis=-1, keepdims=True) + EPS) * g


def _head_rms(v, g, bd, width):
    step = min(width, bd.shape[0])
    outs = []
    for s in range(0, width, step):
        blk = v[:, s:s + step]
        ss = _dot_exact_rhs(blk * blk, bd[:step, :step], 2)
        outs.append(blk * lax.rsqrt(ss * (1.0 / HEAD_DIM) + EPS))
    out = outs[0] if len(outs) == 1 else jnp.concatenate(outs, axis=-1)
    return out * g


def _rot_block(v, cos_t, sin_t):
    lane = lax.broadcasted_iota(jnp.int32, v.shape, 1)
    partner = jnp.where(lane < 80, pltpu.roll(v, LANES - 16, 1), pltpu.roll(v, 16, 1))
    return v * cos_t + partner * sin_t


def _repeat_rows(x, reps):
    return jnp.concatenate([jnp.broadcast_to(x[i:i + 1, :], (reps, x.shape[1])) for i in range(x.shape[0])], axis=0)


def _top_blocks(gate, blk, n_past, n_pick, axis):
    g = jnp.where(blk < n_past, gate, NEG)
    selected = jnp.zeros(gate.shape, jnp.bool_)
    for r in range(n_pick):
        mx = jnp.max(g, axis=axis, keepdims=True)
        idx = jnp.min(jnp.where(g == mx, blk, gate.shape[axis]), axis=axis, keepdims=True)
        hit = blk == idx
        selected = jnp.logical_or(selected, jnp.logical_and(hit, r < n_past))
        g = jnp.where(hit, -3e38, g)
    return selected


def _even_proj_kernel(x_ref, ga_ref, wp_ref, gql_ref, gkv_ref, wuq_ref, gq_ref, gkn_ref, gkr_ref,
                      wuk_ref, wuv_ref, gmq_ref, gmk_ref, bd_ref, cos_ref, sin_ref,
                      c_ref, krb_ref, mk_ref, mv_ref, qmla_ref, kmla_ref, vmla_ref,
                      qmo_ref, kmo_ref, vmo_ref, kmean_ref, *, n_heads, qk_dim, scale):
    x = x_ref[...]
    h = _rms(x, ga_ref[...]).astype(BF16)
    y = jnp.dot(h, wp_ref[...], preferred_element_type=F32)
    cos_t = cos_ref[...]
    sin_t = sin_ref[...]
    bd = bd_ref[...]

    c = _rms(y[:, 256:512], gkv_ref[...])
    c_ref[...] = c
    cb = c.astype(BF16)

    ql = _rms(y[:, 0:256], gql_ref[...]).astype(BF16)
    q = jnp.dot(ql, wuq_ref[...], preferred_element_type=F32)
    gq = gq_ref[...]
    gkn = gkn_ref[...]
    inv_d = 1.0 / qk_dim
    for hh in range(n_heads):
        sl = slice(LANES * hh, LANES * (hh + 1))
        qh = q[:, sl]
        qh = qh * lax.rsqrt(jnp.sum(qh * qh, axis=-1, keepdims=True) * inv_d + EPS) * gq
        qmla_ref[:, sl] = _rot_block(qh * gkn, cos_t, sin_t).astype(BF16)

    krb = y[:, 1280:1408]
    krb_ref[...] = krb
    ssr = jnp.sum(krb * krb, axis=-1, keepdims=True)
    krr = _rot_block(krb * gkr_ref[...], cos_t, sin_t)
    kn = jnp.dot(cb, wuk_ref[...], preferred_element_type=F32)
    for hh in range(n_heads):
        sl = slice(LANES * hh, LANES * (hh + 1))
        kh = kn[:, sl]
        inv = lax.rsqrt((jnp.sum(kh * kh, axis=-1, keepdims=True) + ssr) * inv_d + EPS) * scale
        kmla_ref[:, sl] = ((kh + krr) * inv).astype(BF16)
    vmla_ref[...] = jnp.dot(cb, wuv_ref[...], preferred_element_type=F32).astype(BF16)

    mq = _head_rms(y[:, 512:1024], gmq_ref[...], bd, 512)
    qmo_ref[...] = mq.astype(BF16)
    mk = _head_rms(y[:, 1024:1152], gmk_ref[...], bd, 128)
    mk_ref[...] = mk
    kmo_ref[...] = mk.astype(BF16)
    mv = y[:, 1152:1280]
    mv_ref[...] = mv
    vmo_ref[...] = mv.astype(BF16)
    tm = x.shape[0]
    for i in range(tm // MOBA_BLOCK):
        kmean_ref[0, i:i + 1, :] = jnp.mean(mk[MOBA_BLOCK * i:MOBA_BLOCK * (i + 1)], axis=0, keepdims=True)


def _even_proj(x, ga, wp, gql, gkv, wuq, gq, gkn, gkr, wuk, wuv, gmq, gmk, bd, cos_t, sin_t, *, tm, qk_dim, scale):
    n, d = x.shape
    nt = n // tm
    row = lambda c: pl.BlockSpec((tm, c), lambda i: (i, 0))
    full = lambda a: pl.BlockSpec(a.shape, lambda i: (0,) * a.ndim)
    consts = (ga, wp, gql, gkv, wuq, gq, gkn, gkr, wuk, wuv, gmq, gmk, bd)
    out_shape = (
        jax.ShapeDtypeStruct((n, 256), F32), jax.ShapeDtypeStruct((n, 128), F32),
        jax.ShapeDtypeStruct((n, 128), F32), jax.ShapeDtypeStruct((n, 128), F32),
        jax.ShapeDtypeStruct((n, 1024), BF16), jax.ShapeDtypeStruct((n, 1024), BF16),
        jax.ShapeDtypeStruct((n, 512), BF16), jax.ShapeDtypeStruct((n, 512), BF16),
        jax.ShapeDtypeStruct((n, 128), BF16), jax.ShapeDtypeStruct((n, 128), BF16),
        jax.ShapeDtypeStruct((nt, tm // MOBA_BLOCK, 128), F32),
    )
    out_specs = (row(256), row(128), row(128), row(128), row(1024), row(1024), row(512), row(512),
                 row(128), row(128), pl.BlockSpec((1, tm // MOBA_BLOCK, 128), lambda i: (i, 0, 0)))
    return pl.pallas_call(
        functools.partial(_even_proj_kernel, n_heads=8, qk_dim=qk_dim, scale=scale),
        grid=(nt,),
        in_specs=[row(d)] + [full(a) for a in consts] + [row(128), row(128)],
        out_specs=out_specs, out_shape=out_shape,
        compiler_params=_cp(("parallel",)), name="even_proj",
    )(x, *consts, cos_t, sin_t)


def _odd_proj_kernel(x_ref, ga_ref, wp_ref, bf_ref, gq_ref, gk_ref, bd_ref,
                     q_ref, k_ref, v_ref, lf_ref, kb_ref, vb_ref):
    x = x_ref[...]
    h = _rms(x, ga_ref[...]).astype(BF16)
    y = jnp.dot(h, wp_ref[...], preferred_element_type=F32)
    bd = bd_ref[...]
    q_ref[...] = _head_rms(y[:, 0:1024], gq_ref[...], bd, 1024).astype(BF16)
    k = _head_rms(y[:, 1024:1280], gk_ref[...], bd, 256)
    k_ref[...] = k
    kb_ref[...] = k.astype(BF16)
    v = y[:, 1280:1536]
    v_ref[...] = v
    vb_ref[...] = v.astype(BF16)
    z = y[:, 1536:1664] + bf_ref[...]
    lf_ref[...] = jnp.minimum(z, 0.0) - jnp.log1p(jnp.exp(-jnp.abs(z)))


def _odd_proj(x, ga, wp, bf, gq, gk, bd, *, tm):
    n, d = x.shape
    nt = n // tm
    row = lambda c: pl.BlockSpec((tm, c), lambda i: (i, 0))
    full = lambda a: pl.BlockSpec(a.shape, lambda i: (0,) * a.ndim)
    consts = (ga, wp, bf, gq, gk, bd)
    out_shape = (jax.ShapeDtypeStruct((n, 1024), BF16), jax.ShapeDtypeStruct((n, 256), F32),
                 jax.ShapeDtypeStruct((n, 256), F32), jax.ShapeDtypeStruct((n, 128), F32),
                 jax.ShapeDtypeStruct((n, 256), BF16), jax.ShapeDtypeStruct((n, 256), BF16))
    return pl.pallas_call(
        _odd_proj_kernel, grid=(nt,),
        in_specs=[row(d)] + [full(a) for a in consts],
        out_specs=(row(1024), row(256), row(256), row(128), row(256), row(256)), out_shape=out_shape,
        compiler_params=_cp(("parallel",)), name="odd_proj",
    )(x, *consts)


def _cumsum_kernel(lf_ref, tri_ref, eye_ref, cum_ref, cumt_ref, carry_ref, *, blk):
    carry_ref[...] = jnp.zeros_like(carry_ref)
    s = lf_ref.shape[1]
    tri = tri_ref[...]
    eye = eye_ref[...]
    for i in range(s // blk):
        lf = lf_ref[0, blk * i:blk * (i + 1), :]
        cum = _dot_exact_lhs(tri, lf, 3) + carry_ref[...]
        carry_ref[...] = cum[blk - 1:blk, :]
        cum_ref[0, blk * i:blk * (i + 1), :] = cum
        cumt_ref[0, :, blk * i:blk * (i + 1)] = _dot_nt_exact_lhs(eye, cum, 3)


def _prompt_cumsum(lf):
    b, s, w = lf.shape
    blk = 256
    tri = jnp.tril(jnp.ones((blk, blk), F32)).astype(BF16)
    eye = jnp.eye(w, dtype=BF16)
    return pl.pallas_call(
        functools.partial(_cumsum_kernel, blk=blk), grid=(b,),
        in_specs=[pl.BlockSpec((1, s, w), lambda i: (i, 0, 0)),
                  pl.BlockSpec((blk, blk), lambda i: (0, 0)), pl.BlockSpec((w, w), lambda i: (0, 0))],
        out_specs=(pl.BlockSpec((1, s, w), lambda i: (i, 0, 0)), pl.BlockSpec((1, w, s), lambda i: (i, 0, 0))),
        out_shape=(jax.ShapeDtypeStruct((b, s, w), F32), jax.ShapeDtypeStruct((b, w, s), F32)),
        scratch_shapes=[pltpu.VMEM((1, w), F32)],
        compiler_params=_cp(("parallel",)), name="prompt_cumsum",
    )(lf, tri, eye)


def _bias_tables_kernel(rb_ref, bown_ref, bprev_ref, b31_ref, bnew_ref,
                        town_ref, tprev_ref, t31_ref, tnew_ref, far_ref):
    h = pl.program_id(0)

    def lookup(bucket):
        out = jnp.full(bucket.shape, NEG, F32)
        for b in range(REL_BUCKETS):
            out = jnp.where(bucket == b, rb_ref[b, h] * LOG2E, out)
        return out

    town_ref[0] = lookup(bown_ref[...])
    tprev_ref[0] = lookup(bprev_ref[...])
    t31_ref[0] = lookup(b31_ref[...])
    tnew_ref[0] = lookup(bnew_ref[...])
    far_ref[0] = jnp.full(far_ref.shape[1:], rb_ref[REL_BUCKETS - 1, h] * LOG2E, F32)


def _bias_tables(rel_bias, bown, bprev, b31, bnew):
    n_heads = rel_bias.shape[1]
    full = lambda a: pl.BlockSpec(a.shape, lambda h: (0,) * a.ndim)
    per_head = lambda a: pl.BlockSpec((1,) + a.shape, lambda h: (h,) + (0,) * a.ndim)
    shp = lambda a: jax.ShapeDtypeStruct((n_heads,) + a.shape, F32)
    far = jax.ShapeDtypeStruct((n_heads, 8, LANES), F32)
    return pl.pallas_call(
        _bias_tables_kernel, grid=(n_heads,),
        in_specs=[pl.BlockSpec(memory_space=pltpu.SMEM), full(bown), full(bprev), full(b31), full(bnew)],
        out_specs=(per_head(bown), per_head(bprev), per_head(b31), per_head(bnew),
                   pl.BlockSpec((1, 8, LANES), lambda h: (h, 0, 0))),
        out_shape=(shp(bown), shp(bprev), shp(b31), shp(bnew), far),
        compiler_params=_cp(("parallel",)), name="bias_tables",
    )(rel_bias, bown, bprev, b31, bnew)


def _flash_kernel(*refs, mode, t):
    s_sc = refs[-2:]
    refs = refs[:-2]
    if mode == "mla":
        q_ref, k_ref, v_ref, o_ref, m_sc, acc_sc = refs
    elif mode == "fox":
        q_ref, k_ref, v_ref, cum_ref, cumt_ref, o_ref, m_sc, acc_sc = refs
    else:
        q_ref, k_ref, v_ref, kmean_ref, town_ref, tprev_ref, far_ref, o_ref, m_sc, acc_sc = refs
    hp = pl.program_id(1)
    qi = pl.program_id(2)
    lane = lax.broadcasted_iota(jnp.int32, (t, LANES), 1)
    upper = lane >= 64
    if mode == "mla":
        vhalf = (0, 1)
    else:
        khalf = (hp // 2) % 2
        vhalf = (khalf, khalf)

    q = []
    for a in range(2):
        if mode == "mla":
            q.append(q_ref[:, LANES * a:LANES * (a + 1)])
        else:
            qb = q_ref[...]
            src = jnp.where(khalf == a, qb, pltpu.roll(qb, 64, 1))
            q.append(jnp.where(upper.astype(jnp.int32) == khalf, src, jnp.zeros_like(src)))
    m_sc[...] = jnp.full(m_sc.shape, NEG, F32)
    acc_sc[...] = jnp.zeros(acc_sc.shape, F32)

    if mode == "fox":
        lane_c = lax.broadcasted_iota(jnp.int32, cum_ref.shape, 1)
        qc = [jnp.sum(jnp.where(lane_c == 2 * hp + a, cum_ref[...], 0.0), axis=-1, keepdims=True) * LOG2E
              for a in range(2)]
    if mode == "moba":
        n_sub = 16
        assert k_ref.shape[0] // MOBA_BLOCK <= n_sub
        km = kmean_ref[0, 0:n_sub, :]
        nb = t // MOBA_BLOCK
        blk = lax.broadcasted_iota(jnp.int32, (n_sub, t), 0)
        blk_of_lane = lax.broadcasted_iota(jnp.int32, (n_sub, t), 1) // MOBA_BLOCK
        blk_grp = lax.broadcasted_iota(jnp.int32, (LANES, nb * LANES), 1) // LANES
        blk_row = lax.broadcasted_iota(jnp.int32, (LANES, nb * LANES), 0)
        eye_sub = (lax.broadcasted_iota(jnp.int32, (n_sub, LANES), 0)
                   == lax.broadcasted_iota(jnp.int32, (n_sub, LANES), 1)).astype(BF16)
        own = nb * qi + blk_of_lane
        sel_t = []
        for a in range(2):
            gate_t = sum(_dot_nt(p, q[a]) for p in _split(km, 2))
            selected = _top_blocks(gate_t, blk, own, MOBA_TOPK, axis=0)
            sel_t.append(_dot_tn(jnp.where(selected, 0.0, NEG), eye_sub).astype(BF16))

    row_i = lax.broadcasted_iota(jnp.int32, (t, t), 0)
    col_i = lax.broadcasted_iota(jnp.int32, (t, t), 1)
    causal = row_i >= col_i

    def scores(j, slot):
        rows = pl.ds(pl.multiple_of(j * t, t), t)
        for a in range(2):
            k = k_ref[rows, LANES * a:LANES * (a + 1)] if mode == "mla" else k_ref[rows, :]
            s_sc[slot][a] = _dot_nt(q[a], k)

    def selection(j):
        if mode != "moba":
            return None
        onehot = (blk_row == nb * j + blk_grp).astype(BF16)
        return [jnp.dot(sel_t[a], onehot, preferred_element_type=F32) for a in range(2)]

    def attend(j, slot, kind, sels):
        rows = pl.ds(pl.multiple_of(j * t, t), t)
        v = v_ref[rows, :]
        ones = jnp.ones_like(v)
        for a in range(2):
            s = s_sc[slot][a]
            shift = None
            if mode == "fox":
                s = s - cumt_ref[0, pl.ds(2 * hp + a, 1), rows] * LOG2E
                shift = qc[a]
            if mode == "moba":
                sel = sels[a]
                far = far_ref[a, 0:1, 0:1]
                rep = MOBA_BLOCK // LANES
                if kind == "off":
                    sel = sel + far
                    s = s + jnp.concatenate([sel[:, LANES * (c // rep):LANES * (c // rep + 1)] for c in range(nb * rep)], axis=1)
                else:
                    base = nb if kind == "prev" else 0
                    bias_rows = []
                    for rb in range(nb):
                        quads = []
                        for cb in range(nb):
                            dist = base + rb - cb
                            sq = sel[MOBA_BLOCK * rb:MOBA_BLOCK * (rb + 1), LANES * cb:LANES * (cb + 1)]
                            sq = jnp.concatenate([sq] * rep, axis=1)
                            if dist == 0:
                                quads.append(town_ref[a])
                            elif dist == 1:
                                quads.append(tprev_ref[a] + sq)
                            elif dist >= 2:
                                quads.append(sq + far)
                            else:
                                quads.append(jnp.full((MOBA_BLOCK, MOBA_BLOCK), NEG, F32))
                        bias_rows.append(quads[0] if nb == 1 else jnp.concatenate(quads, axis=1))
                    s = s + (bias_rows[0] if nb == 1 else jnp.concatenate(bias_rows, axis=0))
            elif kind == "diag":
                s = jnp.where(causal, s, NEG)
            m_old = m_sc[a]
            row_max = jnp.max(s, axis=-1, keepdims=True)
            if shift is not None:
                m_new = jnp.maximum(m_old, row_max + shift)
                p = jnp.exp2(s - (m_new - shift))
            else:
                m_new = jnp.maximum(m_old, row_max)
                p = jnp.exp2(s - m_new)
            alpha = jnp.exp2(m_old - m_new)
            if mode == "mla":
                v_a = jnp.where(upper, v, ones) if a else jnp.where(upper, ones, v)
            else:
                v_a = jnp.where(upper.astype(jnp.int32) == khalf, v, ones)
            acc_sc[a] = alpha * acc_sc[a] + jnp.dot(p.astype(BF16), v_a, preferred_element_type=F32)
            m_sc[a] = m_new

    def step(j, slot, kind, ahead=None):
        sels = selection(j)
        if ahead is not None:
            scores(ahead, 1 - slot)
        attend(j, slot, kind, sels)

    def loop_body(i, carry):
        step(2 * i, 0, "off", ahead=2 * i + 1)
        step(2 * i + 1, 1, "off", ahead=2 * i + 2)
        return carry

    n_off = jnp.maximum(qi - 1, 0) if mode == "moba" else qi
    scores(0, 0)
    lax.fori_loop(0, n_off // 2, loop_body, 0)
    first = 2 * (n_off // 2)
    if mode == "moba":
        @pl.when(qi == 0)
        def _():
            step(0, 0, "diag")

        @pl.when(jnp.logical_and(qi >= 1, first == qi - 1))
        def _():
            step(qi - 1, 0, "prev", ahead=qi)
            step(qi, 1, "diag")

        @pl.when(jnp.logical_and(qi >= 1, first == qi - 2))
        def _():
            step(qi - 2, 0, "off", ahead=qi - 1)
            step(qi - 1, 1, "prev", ahead=qi)
            step(qi, 0, "diag")
    else:
        @pl.when(first == qi)
        def _():
            step(qi, 0, "diag")

        @pl.when(first == qi - 1)
        def _():
            step(qi - 1, 0, "off", ahead=qi)
            step(qi, 1, "diag")

    outs = []
    for a in range(2):
        acc = acc_sc[a]
        o = acc * (1.0 / pltpu.roll(acc, 64, 1))
        if mode != "mla":
            o = jnp.where(khalf == a, o, pltpu.roll(o, 64, 1))
        outs.append(o)
    o_ref[...] = jnp.where(upper, outs[1], outs[0]).astype(o_ref.dtype)


def _flash(mode, q, k, v, extra, *, b, s, t):
    nq = s // t
    if mode == "mla":
        n_pairs = v.shape[1] // LANES
        qspec = pl.BlockSpec((t, 2 * LANES), lambda bi, hp, qi: (bi * nq + qi, hp))
        kspec = pl.BlockSpec((s, 2 * LANES), lambda bi, hp, qi: (bi, hp))
        vspec = pl.BlockSpec((s, LANES), lambda bi, hp, qi: (bi, hp))
        extra_specs = []
    else:
        n_pairs = q.shape[1] // LANES
        group_pairs = n_pairs // (k.shape[1] // HEAD_DIM)
        qspec = pl.BlockSpec((t, LANES), lambda bi, hp, qi: (bi * nq + qi, hp))
        kspec = pl.BlockSpec((s, LANES), lambda bi, hp, qi: (bi, hp // (2 * group_pairs)))
        vspec = kspec
        if mode == "fox":
            cum, cumt = extra
            extra_specs = [pl.BlockSpec((None, t, cum.shape[2]), lambda bi, hp, qi: (bi, qi, 0)),
                           pl.BlockSpec((1, cumt.shape[1], s), lambda bi, hp, qi: (bi, 0, 0))]
        else:
            kmean, town, tprev, far = extra
            extra_specs = [pl.BlockSpec((1,) + kmean.shape[1:], lambda bi, hp, qi: (bi, 0, 0)),
                           pl.BlockSpec((2, MOBA_BLOCK, MOBA_BLOCK), lambda bi, hp, qi: (hp, 0, 0)),
                           pl.BlockSpec((2, MOBA_BLOCK, MOBA_BLOCK), lambda bi, hp, qi: (hp, 0, 0)),
                           pl.BlockSpec((2,) + far.shape[1:], lambda bi, hp, qi: (hp, 0, 0))]
    return pl.pallas_call(
        functools.partial(_flash_kernel, mode=mode, t=t),
        grid=(b, n_pairs, nq),
        in_specs=[qspec, kspec, vspec] + extra_specs,
        out_specs=pl.BlockSpec((t, LANES), lambda bi, hp, qi: (bi * nq + qi, hp)),
        out_shape=jax.ShapeDtypeStruct((b * s, n_pairs * LANES), BF16),
        scratch_shapes=[pltpu.VMEM((2, t, 1), F32), pltpu.VMEM((2, t, LANES), F32),
                        pltpu.VMEM((2, t, t), F32), pltpu.VMEM((2, t, t), F32)],
        compiler_params=_cp(("parallel", "parallel", "arbitrary")), name="flash_" + mode,
    )(q, k, v, *extra)


def _page_specs(layer, n_chunk_pages, shape, order, seq=0, n_seq=1):
    specs = []
    for i in range(n_chunk_pages):
        specs.append(pl.BlockSpec(
            (None, None) + shape,
            lambda b, j, pt, i=i: (layer, pt[n_seq * b + seq, order(j) * n_chunk_pages + i], 0, 0)))
    return specs


def _online_softmax(s, m_sc, l_sc, acc_sc, pv):
    m_old = m_sc[...]
    m_new = jnp.maximum(m_old, jnp.max(s, axis=-1, keepdims=True))
    alpha = jnp.exp2(m_old - m_new)
    p = jnp.exp2(s - m_new)
    l_sc[...] = alpha * l_sc[...] + jnp.sum(p, axis=-1, keepdims=True)
    acc_sc[...] = alpha * acc_sc[...] + pv(p)
    m_sc[...] = m_new


def _own_head_lanes(out, rows_per_kv):
    row = lax.broadcasted_iota(jnp.int32, out.shape, 0)
    lane = lax.broadcasted_iota(jnp.int32, out.shape, 1)
    y = jnp.where(lane // HEAD_DIM == row // rows_per_kv, out, 0.0)
    while y.shape[1] > LANES:
        half = y.shape[1] // 2
        y = y[:, :half] + y[:, half:]
    return y + pltpu.roll(y, 64, 1)


def _mla_sample_kernel(pt_ref, *refs, cp, n_chunks, qk_dim, nope, scale, ns):
    shared = refs[2 * ns * cp:]
    per_seq, consts, o_ref, scratch = shared[:4], shared[4:12], shared[12], shared[13:]
    j = pl.program_id(1)

    def run(phase):
        for u in range(ns):
            pages = [refs[(g * ns + u) * cp:(g * ns + u + 1) * cp] for g in range(2)]
            _mla_one_seq(phase, *pages, *[r.at[u] for r in per_seq], *consts, o_ref.at[u], *[r.at[u] for r in scratch],
                         cp=cp, qk_dim=qk_dim, nope=nope, scale=scale)

    pl.when(j == 0)(lambda: run("first"))
    run("chunk")
    pl.when(j == n_chunks - 1)(lambda: run("last"))


def _mla_one_seq(phase, c_refs, kr_refs, qbd_ref, qr_ref, cnew_ref, krnew_ref, cosp_ref, sinp_ref, cosn_ref, sinn_ref,
                 wukt_ref, wuv_ref, gkr_ref, mask_ref, o_ref, m_sc, l_sc, acc_sc, qabs_sc, *, cp, qk_dim, nope, scale):
    tn = cnew_ref.shape[0]
    n_heads = wuv_ref.shape[0]

    if phase == "first":
        m_sc[...] = jnp.full(m_sc.shape, NEG, F32)
        l_sc[...] = jnp.zeros(l_sc.shape, F32)
        acc_sc[...] = jnp.zeros(acc_sc.shape, F32)
        qabs_sc[...] = _dot(qbd_ref[...], wukt_ref[...]).astype(BF16)
        return

    def scores(c, krt, cos_t, sin_t):
        cb = c.astype(BF16)
        knt = _dot_nt(wukt_ref[...], cb)
        sq = knt * knt
        ssr = jnp.sum(krt * krt, axis=0, keepdims=True)
        inv = []
        for hh in range(n_heads):
            ss = jnp.sum(sq[nope * hh:nope * (hh + 1)], axis=0, keepdims=True) + ssr
            inv.append(lax.rsqrt(ss * (1.0 / qk_dim) + EPS) * scale)
        inv = _repeat_rows(jnp.concatenate(inv, axis=0), tn)
        krg = krt * gkr_ref[...]
        half = krt.shape[0] // 2
        krr = krg * cos_t + jnp.concatenate([krg[half:], krg[:half]], axis=0) * sin_t
        return cb, (_dot_nt(qabs_sc[...], cb) + _dot(qr_ref[...], krr)) * inv

    if phase == "last":
        cb, s = scores(cnew_ref[...], krnew_ref[...], cosn_ref[...], sinn_ref[...])
        _online_softmax(jnp.where(mask_ref[...] > 0.5, s, NEG), m_sc, l_sc, acc_sc, lambda p: _dot(p, cb))
        pc = acc_sc[...] * (1.0 / l_sc[...])
        for hh in range(n_heads):
            o_ref[tn * hh:tn * (hh + 1), :] = _dot(pc[tn * hh:tn * (hh + 1), :], wuv_ref[hh])
        return

    n_sub = 4 if cp % 4 == 0 else 1
    pps = cp // n_sub
    cbs, ss = [], []
    for i in range(n_sub):
        c = jnp.concatenate([r[...] for r in c_refs[pps * i:pps * (i + 1)]], axis=0)
        krt = jnp.concatenate([r[...] for r in kr_refs[pps * i:pps * (i + 1)]], axis=1)
        cols = slice(pps * 128 * i, pps * 128 * (i + 1))
        cb, s = scores(c, krt, cosp_ref[:, cols], sinp_ref[:, cols])
        cbs.append(cb)
        ss.append(s)
    cb_all = jnp.concatenate(cbs, axis=0)
    _online_softmax(jnp.concatenate(ss, axis=1), m_sc, l_sc, acc_sc, lambda p: _dot(p, cb_all))


def _mla_sample(page_table, cache_c, cache_krt, layer, qbd, qr, c_new, krt_new, cos_p, sin_p, cos_n, sin_n,
                wukt, wuv, gkr, mask, *, qk_dim, nope, scale):
    db, n_pages = page_table.shape
    cp = min(CHUNK_PAGES, n_pages)
    n_chunks = n_pages // cp
    tn = c_new.shape[1]
    n_heads = wuv.shape[0]
    rows = n_heads * tn
    ns = SEQS_PER_STEP if db % SEQS_PER_STEP == 0 else 1
    fwd = lambda j: j
    per_seq = lambda a: pl.BlockSpec((ns,) + a.shape[1:], lambda b, j, pt: (b,) + (0,) * (a.ndim - 1))
    full = lambda a: pl.BlockSpec(a.shape, lambda b, j, pt: (0,) * a.ndim)
    chunk_cols = lambda a: pl.BlockSpec((a.shape[0], cp * 128), lambda b, j, pt: (0, j))
    in_specs = []
    for cache in (cache_c, cache_krt):
        for u in range(ns):
            in_specs += _page_specs(layer, cp, cache.shape[2:], fwd, u, ns)
    in_specs += [per_seq(qbd), per_seq(qr), per_seq(c_new), per_seq(krt_new), chunk_cols(cos_p), chunk_cols(sin_p),
                 full(cos_n), full(sin_n), full(wukt), full(wuv), full(gkr), full(mask)]
    lat = cache_c.shape[-1]
    grid_spec = pltpu.PrefetchScalarGridSpec(
        num_scalar_prefetch=1, grid=(db // ns, n_chunks), in_specs=in_specs,
        out_specs=pl.BlockSpec((ns, rows, wuv.shape[2]), lambda b, j, pt: (b, 0, 0)),
        scratch_shapes=[pltpu.VMEM((ns, rows, 1), F32), pltpu.VMEM((ns, rows, 1), F32), pltpu.VMEM((ns, rows, lat), F32),
                        pltpu.VMEM((ns, rows, lat), BF16)])
    return pl.pallas_call(
        functools.partial(_mla_sample_kernel, cp=cp, n_chunks=n_chunks, qk_dim=qk_dim, nope=nope, scale=scale, ns=ns),
        grid_spec=grid_spec,
        out_shape=jax.ShapeDtypeStruct((db, rows, wuv.shape[2]), F32),
        compiler_params=_cp(("parallel", "arbitrary")), name="mla_sample",
    )(page_table, *([cache_c] * (cp * ns)), *([cache_krt] * (cp * ns)), qbd, qr, c_new, krt_new, cos_p, sin_p, cos_n, sin_n,
      wukt, wuv, gkr, mask)


def _moba_sample_kernel(pt_ref, *refs, cp, n_chunks, n_blocks, rows_per_kv):
    k_refs = refs[:cp]
    v_refs = refs[cp:2 * cp]
    (q_ref, knew_ref, vnew_ref, t31_ref, tnew_ref, far_ref, ind_ref, expand_ref, o_ref,
     kt_all, vt_all, g_all) = refs[2 * cp:]
    j = pl.program_id(1)
    q = q_ref[0]
    bpc = cp * 128 // MOBA_BLOCK
    tk = cp * 128

    kt = jnp.concatenate([r[...] for r in k_refs], axis=1)
    vt = jnp.concatenate([r[...] for r in v_refs], axis=1)
    cols = pl.ds(pl.multiple_of(j * tk, tk), tk)
    kt_all[:, cols] = kt.astype(BF16)
    vt_all[:, cols] = vt.astype(BF16)
    kmean = _dot_exact_rhs(kt, ind_ref[...], 2) * (1.0 / MOBA_BLOCK)
    gate = pltpu.roll(_dot_exact_lhs(q, kmean, 2), j * bpc, 1)

    @pl.when(j == 0)
    def _():
        g_all[...] = gate

    @pl.when(j > 0)
    def _():
        g_all[...] += gate

    @pl.when(j == n_chunks - 1)
    def _():
        lane = lax.broadcasted_iota(jnp.int32, g_all.shape, 1)
        picked = _top_blocks(g_all[...], lane, n_blocks, min(MOBA_TOPK, n_blocks + 1), axis=1)
        selb = jnp.where(picked, 0.0, NEG).astype(BF16)
        s = _dot(q, kt_all[...]) + jnp.dot(selb, expand_ref[...], preferred_element_type=F32)
        last = s.shape[1] - MOBA_BLOCK
        s = jnp.concatenate([s[:, :last] + far_ref[...], s[:, last:] + t31_ref[...]], axis=1)
        s_new = _dot_nt(q, knew_ref[0]) + tnew_ref[...]
        m = jnp.maximum(jnp.max(s, axis=-1, keepdims=True), jnp.max(s_new, axis=-1, keepdims=True))
        p = jnp.exp2(s - m)
        p_new = jnp.exp2(s_new - m)
        l = jnp.sum(p, axis=-1, keepdims=True) + jnp.sum(p_new, axis=-1, keepdims=True)
        out = (_dot_nt(p, vt_all[...]) + _dot(p_new, vnew_ref[0])) * (1.0 / l)
        o_ref[0] = _own_head_lanes(out, rows_per_kv)


def _moba_sample(page_table, cache_kt, cache_vt, layer, q, k_new, v_new, t31, tnew, far, *, rows_per_kv):
    db, n_pages = page_table.shape
    cp = min(CHUNK_PAGES, n_pages)
    n_chunks = n_pages // cp
    past = n_pages * 128
    n_blocks = past // MOBA_BLOCK
    assert n_blocks <= LANES
    rows = q.shape[1]
    key_blk = jnp.arange(past) // MOBA_BLOCK
    ind = (key_blk[:cp * 128, None] == jnp.arange(LANES)[None, :]).astype(BF16)
    expand = (jnp.arange(LANES)[:, None] == key_blk[None, :]).astype(BF16)
    fwd = lambda j: j
    per_seq = lambda a: pl.BlockSpec((1,) + a.shape[1:], lambda b, j, pt: (b,) + (0,) * (a.ndim - 1))
    full = lambda a: pl.BlockSpec(a.shape, lambda b, j, pt: (0,) * a.ndim)
    in_specs = (_page_specs(layer, cp, cache_kt.shape[2:], fwd) + _page_specs(layer, cp, cache_vt.shape[2:], fwd)
                + [per_seq(q), per_seq(k_new), per_seq(v_new), full(t31), full(tnew), full(far), full(ind), full(expand)])
    grid_spec = pltpu.PrefetchScalarGridSpec(
        num_scalar_prefetch=1, grid=(db, n_chunks), in_specs=in_specs,
        out_specs=pl.BlockSpec((1, rows, LANES), lambda b, j, pt: (b, 0, 0)),
        scratch_shapes=[pltpu.VMEM((cache_kt.shape[2], past), BF16), pltpu.VMEM((cache_vt.shape[2], past), BF16),
                        pltpu.VMEM((rows, LANES), F32)])
    return pl.pallas_call(
        functools.partial(_moba_sample_kernel, cp=cp, n_chunks=n_chunks, n_blocks=n_blocks, rows_per_kv=rows_per_kv),
        grid_spec=grid_spec, out_shape=jax.ShapeDtypeStruct((db, rows, LANES), F32),
        compiler_params=_cp(("parallel", "arbitrary")), name="moba_sample",
    )(page_table, *([cache_kt] * cp), *([cache_vt] * cp), q, k_new, v_new, t31, tnew, far, ind, expand)


def _fox_sample_kernel(pt_ref, *refs, cp, n_chunks, rows_per_kv, ns):
    shared = refs[3 * ns * cp:]
    per_seq, consts, o_ref, scratch = shared[:4], shared[4:8], shared[8], shared[9:]
    j = pl.program_id(1)

    def seq_args(u):
        pages = [refs[(g * ns + u) * cp:(g * ns + u + 1) * cp] for g in range(3)]
        return (*pages, *[r.at[u] for r in per_seq], *consts, o_ref.at[u], *[r.at[u] for r in scratch])

    @pl.when(j == 0)
    def _():
        for u in range(ns):
            _fox_one_seq("first", *seq_args(u), rows_per_kv=rows_per_kv)

    for u in range(ns):
        _fox_one_seq("chunk", *seq_args(u), rows_per_kv=rows_per_kv)

    @pl.when(j == n_chunks - 1)
    def _():
        for u in range(ns):
            _fox_one_seq("last", *seq_args(u), rows_per_kv=rows_per_kv)


def _fox_one_seq(phase, k_refs, v_refs, lf_refs, q_ref, knew_ref, vnew_ref, lftnew_ref, rep_ref, tri_ref, tmask_ref,
                 cmask_ref, o_ref, m_sc, l_sc, acc_sc, carry_sc, ncol_sc, *, rows_per_kv):
    q = q_ref[...]
    n_h = lftnew_ref.shape[0]
    reps = q.shape[0] // n_h

    if phase == "last":
        o_ref[...] = _own_head_lanes(acc_sc[...] * (1.0 / l_sc[...]), rows_per_kv)
        return
    if phase == "first":
        m_sc[...] = jnp.full(m_sc.shape, NEG, F32)
        l_sc[...] = jnp.zeros(l_sc.shape, F32)
        acc_sc[...] = jnp.zeros(acc_sc.shape, F32)
        carry_sc[...] = jnp.zeros(carry_sc.shape, F32)
        n_t = _dot_exact_rhs(lftnew_ref[...] * LOG2E, tri_ref[...], 3)
        nmat = _dot_exact_lhs(rep_ref[...], n_t, 3)
        ncol = jnp.sum(nmat * tmask_ref[...], axis=-1, keepdims=True)
        ncol_sc[...] = ncol
        s = _dot_nt(q, knew_ref[...]) + (ncol - nmat)
        _online_softmax(jnp.where(cmask_ref[...] > 0.5, s, NEG), m_sc, l_sc, acc_sc, lambda p: _dot(p, vnew_ref[...]))
        return

    kt = jnp.concatenate([r[...] for r in k_refs], axis=1)
    vt = jnp.concatenate([r[...] for r in v_refs], axis=1)
    lft = jnp.concatenate([r[...] for r in lf_refs], axis=1) * LOG2E
    tk = lft.shape[1]
    lane = lax.broadcasted_iota(jnp.int32, lft.shape, 1)
    x = lft
    sh = 1
    while sh < tk:
        if sh % LANES == 0:
            shifted = jnp.concatenate([x[:, sh:], jnp.zeros((n_h, sh), F32)], axis=1)
        else:
            shifted = jnp.where(lane < tk - sh, pltpu.roll(x, tk - sh, 1), 0.0)
        x = x + shifted
        sh *= 2
    carry = carry_sc[...]
    later = x - lft + carry
    carry_sc[...] = carry + x[:, 0:1]
    s = _dot(q, kt) + _repeat_rows(later, reps) + ncol_sc[...]
    _online_softmax(s, m_sc, l_sc, acc_sc, lambda p: _dot_nt(p, vt))


def _fox_sample(page_table, cache_kt, cache_vt, cache_lft, layer, q, k_new, v_new, lft_new, rep, tri, tmask, cmask,
                *, rows_per_kv):
    db, n_pages = page_table.shape
    cp = min(CHUNK_PAGES, n_pages)
    n_chunks = n_pages // cp
    rows = q.shape[1]
    ns = SEQS_PER_STEP if db % SEQS_PER_STEP == 0 else 1
    rev = lambda j: n_chunks - 1 - j
    per_seq = lambda a: pl.BlockSpec((ns,) + a.shape[1:], lambda b, j, pt: (b,) + (0,) * (a.ndim - 1))
    full = lambda a: pl.BlockSpec(a.shape, lambda b, j, pt: (0,) * a.ndim)
    in_specs = []
    for cache in (cache_kt, cache_vt, cache_lft):
        for u in range(ns):
            in_specs += _page_specs(layer, cp, cache.shape[2:], rev, u, ns)
    in_specs += [per_seq(q), per_seq(k_new), per_seq(v_new), per_seq(lft_new), full(rep), full(tri), full(tmask), full(cmask)]
    grid_spec = pltpu.PrefetchScalarGridSpec(
        num_scalar_prefetch=1, grid=(db // ns, n_chunks), in_specs=in_specs,
        out_specs=pl.BlockSpec((ns, rows, LANES), lambda b, j, pt: (b, 0, 0)),
        scratch_shapes=[pltpu.VMEM((ns, rows, 1), F32), pltpu.VMEM((ns, rows, 1), F32), pltpu.VMEM((ns, rows, q.shape[2]), F32),
                        pltpu.VMEM((ns, lft_new.shape[1], 1), F32), pltpu.VMEM((ns, rows, 1), F32)])
    return pl.pallas_call(
        functools.partial(_fox_sample_kernel, cp=cp, n_chunks=n_chunks, rows_per_kv=rows_per_kv, ns=ns),
        grid_spec=grid_spec, out_shape=jax.ShapeDtypeStruct((db, rows, LANES), F32),
        compiler_params=_cp(("parallel", "arbitrary")), name="fox_sample",
    )(page_table, *([cache_kt] * (cp * ns)), *([cache_vt] * (cp * ns)), *([cache_lft] * (cp * ns)), q, k_new, v_new, lft_new,
      rep, tri, tmask, cmask)


def _mix_mlp_kernel(x_ref, o_ref, wo_ref, g_ref, up_ref, down_ref, y_ref, x1_sc, h_sc, acc_sc):
    f = pl.program_id(1)

    @pl.when(f == 0)
    def _():
        x1 = x_ref[...] + jnp.dot(o_ref[...], wo_ref[...], preferred_element_type=F32)
        x1_sc[...] = x1
        h_sc[...] = _rms(x1, g_ref[...]).astype(BF16)
        acc_sc[...] = jnp.zeros(acc_sc.shape, F32)

    u = jnp.maximum(jnp.dot(h_sc[...], up_ref[...], preferred_element_type=F32), 0.0)
    acc_sc[...] += jnp.dot((u * u).astype(BF16), down_ref[...], preferred_element_type=F32)

    @pl.when(f == pl.num_programs(1) - 1)
    def _():
        y_ref[...] = x1_sc[...] + acc_sc[...]


def _mix_mlp(x, o, wo, g, up, down, *, tm):
    n, d = x.shape
    dff = up.shape[1]
    tf = min(FF_TILE, dff)
    return pl.pallas_call(
        _mix_mlp_kernel, grid=(n // tm, dff // tf),
        in_specs=[pl.BlockSpec((tm, d), lambda i, f: (i, 0)), pl.BlockSpec((tm, o.shape[1]), lambda i, f: (i, 0)),
                  pl.BlockSpec(wo.shape, lambda i, f: (0, 0)), pl.BlockSpec(g.shape, lambda i, f: (0, 0)),
                  pl.BlockSpec((d, tf), lambda i, f: (0, f)), pl.BlockSpec((tf, d), lambda i, f: (f, 0))],
        out_specs=pl.BlockSpec((tm, d), lambda i, f: (i, 0)),
        out_shape=jax.ShapeDtypeStruct((n, d), F32),
        scratch_shapes=[pltpu.VMEM((tm, d), F32), pltpu.VMEM((tm, d), BF16), pltpu.VMEM((tm, d), F32)],
        compiler_params=_cp(("parallel", "arbitrary")), name="mix_mlp",
    )(x, o, wo, g, up, down)


def _t5_bucket(rel):
    n = jnp.maximum(rel, 0)
    exact = REL_BUCKETS // 2
    scaled = jnp.log(jnp.maximum(n, 1).astype(F32) / exact) / math.log(REL_MAX_DIST / exact)
    large = exact + (scaled * (REL_BUCKETS - exact)).astype(jnp.int32)
    return jnp.where(n < exact, n, jnp.minimum(large, REL_BUCKETS - 1))


def _rope_tables(pos, half):
    inv = ROPE_THETA ** (-jnp.arange(half, dtype=F32) / half)
    ang = pos.astype(F32)[:, None] * inv[None, :]
    return jnp.cos(ang), jnp.sin(ang)


def _pick_tile(n, cap):
    t = cap
    while n % t:
        t //= 2
    return t


def _pages_on_lanes(cache):
    nd = cache.ndim
    t = jnp.transpose(cache, (0, 1) + tuple(range(3, nd)) + (2,))
    return t.reshape(t.shape[:2] + (-1, t.shape[-1]))


def kernel(x_prompt, x_sample, cache_mla_latent, cache_mla_krope, cache_moba_k, cache_moba_v, cache_fox_k, cache_fox_v, cache_fox_logf, page_table, rel_bias, g_attn, g_mlp, w_mlp_up, w_mlp_down, w_in_even, g_q_lat, g_kv_lat, w_uq, w_uk, w_uv, g_mla_q, g_mla_k, g_moba_q, g_moba_k, w_o_even, w_in_odd, b_forget, g_fox_q, g_fox_k, w_o_odd):
    B, S, D = x_prompt.shape
    DB, TN, _ = x_sample.shape
    depth = g_attn.shape[0]
    n_pages = page_table.shape[1]
    page = cache_mla_latent.shape[2]
    past = n_pages * page
    n_p, n_s = B * S, DB * TN
    n = n_p + n_s
    q_lora, kv_lora = g_q_lat.shape[1], g_kv_lat.shape[1]
    mla_heads, nope, mla_v = w_uk.shape[2], w_uk.shape[3], w_uv.shape[3]
    qk_dim = w_uq.shape[3]
    rope_d = qk_dim - nope
    half = rope_d // 2
    moba_kv = cache_moba_k.shape[3]
    moba_heads = rel_bias.shape[1]
    moba_g = moba_heads // moba_kv
    fox_heads = b_forget.shape[1]
    fox_kv = cache_fox_k.shape[3]
    fox_g = fox_heads // fox_kv
    mla_scale = float(qk_dim) ** -0.5
    att_scale = HEAD_DIM ** -0.5
    assert (q_lora, kv_lora, mla_heads, nope, rope_d, mla_v) == (256, 256, 8, 64, 32, 64)
    assert (moba_heads, moba_kv, fox_heads, fox_kv, page, TN) == (8, 2, 16, 4, 128, 8)
    assert past % MOBA_BLOCK == 0 and S % 512 == 0 and n_p % 256 == 0 and n_s % 256 == 0
    assert MOBA_BLOCK + 1 >= REL_MAX_DIST
    tm = _pick_tile(math.gcd(n_p, n_s), TOKEN_TILE)

    pos = jnp.concatenate([jnp.tile(jnp.arange(S, dtype=jnp.int32), B),
                           jnp.tile(past + jnp.arange(TN, dtype=jnp.int32), DB)])
    cos, sin = _rope_tables(pos, half)
    ones64 = jnp.ones((n, 64), F32)
    zeros32 = jnp.zeros((n, 32), F32)
    cos_t = jnp.concatenate([ones64, cos, cos, zeros32], axis=1)
    sin_t = jnp.concatenate([jnp.zeros((n, 64), F32), -sin, sin, zeros32], axis=1)
    kcos, ksin = _rope_tables(jnp.arange(past + TN, dtype=jnp.int32), half)
    kcos_t = jnp.concatenate([kcos, kcos], axis=1).T
    ksin_t = jnp.concatenate([-ksin, ksin], axis=1).T

    bd = (jnp.arange(512)[:, None] // HEAD_DIM == jnp.arange(512)[None, :] // HEAD_DIM).astype(BF16)
    t_new = jnp.arange(TN)
    t_mla = jnp.arange(mla_heads * TN) % TN
    mla_mask = (t_new[None, :] <= t_mla[:, None]).astype(F32)
    r_fox = jnp.arange(fox_heads * TN)
    rep_fox = (r_fox[:, None] // TN == jnp.arange(fox_heads)[None, :]).astype(BF16)
    tri_new = (t_new[:, None] <= t_new[None, :]).astype(BF16)
    tmask_fox = (t_new[None, :] == (r_fox % TN)[:, None]).astype(F32)
    cmask_fox = (t_new[None, :] <= (r_fox % TN)[:, None]).astype(F32)

    ii = jnp.arange(MOBA_BLOCK)
    rel_own = ii[:, None] - ii[None, :]
    b_own = jnp.where(rel_own >= 0, _t5_bucket(rel_own), -1)
    b_prev = _t5_bucket(rel_own + MOBA_BLOCK)
    b_31 = _t5_bucket(MOBA_BLOCK + t_new[:, None] - ii[None, :])
    rel_new = t_new[:, None] - t_new[None, :]
    b_new = jnp.where(rel_new >= 0, _t5_bucket(rel_new), -1)
    b_new = jnp.pad(b_new, ((0, 0), (0, LANES - TN)), constant_values=-1)
    town, tprev, t31, tnew, far = _bias_tables(rel_bias.astype(F32), b_own.astype(jnp.int32), b_prev.astype(jnp.int32),
                                               b_31.astype(jnp.int32), b_new.astype(jnp.int32))
    t31_s = t31.reshape(moba_heads * TN, MOBA_BLOCK)
    tnew_s = tnew.reshape(moba_heads * TN, LANES)[:, :TN]
    far_s = far.reshape(moba_heads * 8, LANES)[:, :1]

    krope_t = _pages_on_lanes(cache_mla_krope)
    moba_kt, moba_vt = _pages_on_lanes(cache_moba_k), _pages_on_lanes(cache_moba_v)
    fox_kt, fox_vt = _pages_on_lanes(cache_fox_k), _pages_on_lanes(cache_fox_v)
    fox_lft = _pages_on_lanes(cache_fox_logf)

    pt = page_table.astype(jnp.int32)
    x = jnp.concatenate([x_prompt.reshape(n_p, D), x_sample.reshape(n_s, D)], axis=0)
    outs = {k: [] for k in ("lat", "kr", "mk", "mv", "fk", "fv", "lf")}

    for layer in range(depth):
        ga = g_attn[layer][None, :]
        if layer % 2 == 0:
            e = layer // 2
            w = w_in_even[e]
            wp = jnp.concatenate([w[:, 0:512], w[:, 544:1312], jnp.zeros((D, nope), F32), w[:, 512:544],
                                  jnp.zeros((D, LANES - qk_dim), F32)], axis=1).astype(BF16)
            wuq = jnp.pad(w_uq[e], ((0, 0), (0, 0), (0, LANES - qk_dim))).reshape(q_lora, mla_heads * LANES).astype(BF16)
            wuk_pad = jnp.pad(w_uk[e], ((0, 0), (0, 0), (0, LANES - nope))).reshape(kv_lora, mla_heads * LANES).astype(BF16)
            wukt = w_uk[e].transpose(1, 2, 0).reshape(mla_heads * nope, kv_lora).astype(BF16)
            wuv = w_uv[e].reshape(kv_lora, mla_heads * mla_v).astype(BF16)
            wuv_h = w_uv[e].transpose(1, 0, 2).astype(BF16)
            gq = jnp.pad(g_mla_q[e], (0, LANES - qk_dim))[None, :]
            gkn = jnp.concatenate([g_mla_k[e][:nope], jnp.ones((LANES - nope,), F32)])[None, :]
            gkr = jnp.concatenate([jnp.zeros((nope,), F32), g_mla_k[e][nope:], jnp.zeros((LANES - qk_dim,), F32)])[None, :]
            gmq = (jnp.tile(g_moba_q[e], moba_heads) * (att_scale * LOG2E))[None, :]
            gmk = jnp.tile(g_moba_k[e], moba_kv)[None, :]
            (c, krb, mk, mv, qmla, kmla, vmla, qmo, kmo, vmo, kmean) = _even_proj(
                x, ga, wp, g_q_lat[e][None, :], g_kv_lat[e][None, :], wuq, gq, gkn, gkr, wuk_pad, wuv, gmq, gmk, bd,
                cos_t, sin_t, tm=tm, qk_dim=qk_dim, scale=mla_scale * LOG2E)
            kr = krb[:, 64:64 + rope_d]
            o1_p = _flash("mla", qmla, kmla, vmla, (), b=B, s=S, t=512)
            nblk = S // MOBA_BLOCK
            km_p = kmean.reshape(n // MOBA_BLOCK, 128)[:B * nblk].reshape(B, nblk, 128)
            km_p = jnp.pad(km_p, ((0, 0), (0, LANES - nblk), (0, 0)))
            o2_p = _flash("moba", qmo, kmo, vmo, (km_p, town, tprev, far), b=B, s=S, t=512)
            qs = qmla[n_p:].reshape(DB, TN, mla_heads, LANES).transpose(0, 2, 1, 3)
            qbd = jnp.einsum('bhtd,hj->bhtjd', qs[..., :nope], jnp.eye(mla_heads, dtype=BF16))
            qbd = qbd.reshape(DB, mla_heads * TN, mla_heads * nope)
            qr = qs[..., nope:qk_dim].reshape(DB, mla_heads * TN, rope_d)
            o1_s = _mla_sample(pt, cache_mla_latent, krope_t, e, qbd, qr,
                               c[n_p:].reshape(DB, TN, kv_lora), kr[n_p:].reshape(DB, TN, rope_d).transpose(0, 2, 1),
                               kcos_t[:, :past], ksin_t[:, :past], kcos_t[:, past:], ksin_t[:, past:],
                               wukt, wuv_h, g_mla_k[e][nope:][:, None], mla_mask,
                               qk_dim=qk_dim, nope=nope, scale=mla_scale * LOG2E)
            o1_s = o1_s.reshape(DB, mla_heads, TN, mla_v).transpose(0, 2, 1, 3).reshape(n_s, mla_heads * mla_v)
            q2 = qmo[n_p:].reshape(DB, TN, moba_kv, moba_g, HEAD_DIM).transpose(0, 2, 3, 1, 4)
            q2 = jnp.einsum('bkgtd,kj->bkgtjd', q2, jnp.eye(moba_kv, dtype=BF16)).reshape(DB, moba_heads * TN, moba_kv * HEAD_DIM)
            o2_s = _moba_sample(pt, moba_kt, moba_vt, e, q2, mk[n_p:].reshape(DB, TN, -1), mv[n_p:].reshape(DB, TN, -1),
                                t31_s, tnew_s, far_s, rows_per_kv=moba_g * TN)
            o2_s = o2_s[:, :, :HEAD_DIM].reshape(DB, moba_heads, TN, HEAD_DIM).transpose(0, 2, 1, 3)
            o2_s = o2_s.reshape(n_s, moba_heads * HEAD_DIM)
            o = jnp.concatenate([jnp.concatenate([o1_p, o2_p], axis=1),
                                 jnp.concatenate([o1_s.astype(BF16), o2_s.astype(BF16)], axis=1)], axis=0)
            wo = w_o_even[e].astype(BF16)
            outs["lat"].append(c); outs["kr"].append(kr); outs["mk"].append(mk); outs["mv"].append(mv)
        else:
            jl = layer // 2
            w = w_in_odd[jl]
            wp = jnp.concatenate([w, jnp.zeros((D, LANES - fox_heads), F32)], axis=1).astype(BF16)
            bf = jnp.pad(b_forget[jl], (0, LANES - fox_heads))[None, :]
            gq = (jnp.tile(g_fox_q[jl], fox_heads) * (att_scale * LOG2E))[None, :]
            gk = jnp.tile(g_fox_k[jl], fox_kv)[None, :]
            q, k, v, lf, kb, vb = _odd_proj(x, ga, wp, bf, gq, gk, bd, tm=tm)
            cum, cumt = _prompt_cumsum(lf[:n_p].reshape(B, S, LANES))
            o_p = _flash("fox", q, kb, vb, (cum, cumt), b=B, s=S, t=512)
            qf = q[n_p:].reshape(DB, TN, fox_kv, fox_g, HEAD_DIM).transpose(0, 2, 3, 1, 4)
            qf = jnp.einsum('bkgtd,kj->bkgtjd', qf, jnp.eye(fox_kv, dtype=BF16)).reshape(DB, fox_heads * TN, fox_kv * HEAD_DIM)
            lfh = lf[:, :fox_heads]
            o_s = _fox_sample(pt, fox_kt, fox_vt, fox_lft, jl, qf,
                              k[n_p:].reshape(DB, TN, -1), v[n_p:].reshape(DB, TN, -1),
                              lfh[n_p:].reshape(DB, TN, fox_heads).transpose(0, 2, 1),
                              rep_fox, tri_new, tmask_fox, cmask_fox, rows_per_kv=fox_g * TN)
            o_s = o_s[:, :, :HEAD_DIM].reshape(DB, fox_heads, TN, HEAD_DIM).transpose(0, 2, 1, 3).reshape(n_s, fox_heads * HEAD_DIM)
            o = jnp.concatenate([o_p, o_s.astype(BF16)], axis=0)
            wo = w_o_odd[jl].astype(BF16)
            outs["fk"].append(k); outs["fv"].append(v); outs["lf"].append(lfh)
        x = _mix_mlp(x, o, wo, g_mlp[layer][None, :], w_mlp_up[layer].astype(BF16), w_mlp_down[layer].astype(BF16), tm=tm)

    def split(name, tail):
        a = jnp.stack(outs[name])
        return a[:, :n_p].reshape((a.shape[0], B, S) + tail), a[:, n_p:].reshape((a.shape[0], DB, TN) + tail)

    lat_p, lat_s = split("lat", (kv_lora,))
    kr_p, kr_s = split("kr", (rope_d,))
    mk_p, mk_s = split("mk", (moba_kv, HEAD_DIM))
    mv_p, mv_s = split("mv", (moba_kv, HEAD_DIM))
    fk_p, fk_s = split("fk", (fox_kv, HEAD_DIM))
    fv_p, fv_s = split("fv", (fox_kv, HEAD_DIM))
    lf_p, lf_s = split("lf", (fox_heads,))
    return (x[:n_p].reshape(B, S, D), x[n_p:].reshape(DB, TN, D), lat_p, lat_s, kr_p, kr_s, mk_p, mk_s, mv_p, mv_s,
            fk_p, fk_s, fv_p, fv_s, lf_p, lf_s)
```
